```python
import math
import jax, jax.numpy as jnp
from jax import lax
import numpy as np

D_MODEL = 2048
BATCH = 1
SEQ = 8192
DEPTH = 2
DEC_BATCH = 128
DEC_SEQ = 4
PAST_LEN = 16384
PAGE_SIZE = 128

N_A_LAYERS = DEPTH // 2
N_B_LAYERS = DEPTH - N_A_LAYERS
HEAD_DIM = 128
DSWA_GROUPS = ((128, 1), (512, 4), (2048, 16))
N_GROUPS = len(DSWA_GROUPS)
HEADS_PER_GROUP = 4
N_A_HEADS = N_GROUPS * HEADS_PER_GROUP
A_QKV_WIDTH = 3 * N_A_HEADS * HEAD_DIM
A_OUT_WIDTH = HEADS_PER_GROUP * HEAD_DIM
N_MEM = 256
N_MEM_HEADS = 4
MEM_WIDTH = N_MEM_HEADS * HEAD_DIM
N_B_HEADS = 12
Q_LORA_RANK = 1536
KV_LORA_RANK = 512
QK_NOPE_DIM = 128
QK_ROPE_DIM = 64
V_HEAD_DIM = 128
MLA_ROW = KV_LORA_RANK + QK_ROPE_DIM
MLA_SCALE = (QK_NOPE_DIM + QK_ROPE_DIM) ** -0.5
ROPE_THETA = 10000.0
N_BUCKETS = 32
MAX_DISTANCE = 2048
D_FF = 4 * D_MODEL
EPS = 1e-6
Q_BLOCK = 128

kernel_name = 'yoco_dilated_swa_mla_decoder_step'


def rms_norm(x, g):
    xf = x.astype(jnp.float32)
    y = xf * lax.rsqrt(jnp.mean(xf * xf, axis=-1, keepdims=True) + EPS)
    return (y * g.astype(jnp.float32)).astype(x.dtype)


def rope(x, pos):
    half = QK_ROPE_DIM // 2
    inv_freq = ROPE_THETA ** (-jnp.arange(half, dtype=jnp.float32) / half)
    ang = pos.astype(jnp.float32)[:, None] * inv_freq[None, :]
    cos = jnp.cos(ang)[:, None, :]
    sin = jnp.sin(ang)[:, None, :]
    xf = x.astype(jnp.float32)
    x1, x2 = xf[..., :half], xf[..., half:]
    return jnp.concatenate([x1 * cos - x2 * sin, x1 * sin + x2 * cos], axis=-1).astype(x.dtype)


def t5_bucket(dist):
    max_exact = N_BUCKETS // 2
    d = jnp.maximum(dist, 1).astype(jnp.float32)
    large = max_exact + (jnp.log(d / max_exact) / math.log(MAX_DISTANCE / max_exact)
                         * (N_BUCKETS - max_exact)).astype(jnp.int32)
    large = jnp.minimum(large, N_BUCKETS - 1)
    return jnp.where(dist < max_exact, dist, large)


def group_bias(t5_bias, g):
    w, d = DSWA_GROUPS[g]
    offs = jnp.arange(w // d + 1, dtype=jnp.int32) * d
    return t5_bias[t5_bucket(offs), g * HEADS_PER_GROUP:(g + 1) * HEADS_PER_GROUP]


def dilated_attend(q, kv_src, q_pos, src_start, window, dil, bias):
    offs = jnp.arange(window // dil + 1, dtype=jnp.int32) * dil
    pos = q_pos[:, None] - offs[None, :]
    valid = pos >= 0
    rows = jnp.clip(pos - src_start, 0, kv_src.shape[1] - 1)
    kv = kv_src[:, rows]
    s = jnp.einsum('bthd,btjhd->bthj', q, kv[:, :, :, 0], preferred_element_type=jnp.float32)
    s = s * (HEAD_DIM ** -0.5) + bias.astype(jnp.float32).T[None, None]
    s = jnp.where(valid[None, :, None, :], s, -jnp.inf)
    m = jnp.max(s, axis=-1)
    e = jnp.exp(s - m[..., None])
    l = jnp.sum(e, axis=-1)
    acc = jnp.einsum('bthj,btjhd->bthd', e, kv[:, :, :, 1].astype(jnp.float32))
    return acc, m, l


def dswa_mix(q, srcs, q_pos, src_starts, t5_bias):
    parts = [dilated_attend(q[:, :, g], srcs[g], q_pos, src_starts[g], w, d, group_bias(t5_bias, g))
             for g, (w, d) in enumerate(DSWA_GROUPS)]
    m_all = jnp.max(jnp.stack([p[1] for p in parts]), axis=0)
    num = sum(jnp.exp(m - m_all)[..., None] * acc for acc, m, _ in parts)
    den = sum(jnp.exp(m - m_all) * l for _, m, l in parts)
    return (num / den[..., None]).astype(q.dtype)


def dswa_prompt(q, k, v, t5_bias):
    B, S = q.shape[:2]
    srcs = [jnp.stack([k[:, :, g], v[:, :, g]], axis=2) for g in range(N_GROUPS)]
    n_blk = S // Q_BLOCK
    qb = jnp.moveaxis(q.reshape(B, n_blk, Q_BLOCK, N_GROUPS, HEADS_PER_GROUP, HEAD_DIM), 1, 0)

    def block(args):
        q_blk, i = args
        q_pos = i * Q_BLOCK + jnp.arange(Q_BLOCK, dtype=jnp.int32)
        return dswa_mix(q_blk, srcs, q_pos, (0,) * N_GROUPS, t5_bias)

    out = lax.map(block, (qb, jnp.arange(n_blk, dtype=jnp.int32)))
    out = jnp.moveaxis(out, 0, 1).reshape(B, S, HEADS_PER_GROUP, HEAD_DIM)
    states = [src[:, S - min(w, S):] for src, (w, _) in zip(srcs, DSWA_GROUPS)]
    return out, states


def dswa_sample(q, k, v, bufs, q_pos, t5_bias):
    srcs, starts, states = [], [], []
    for g, buf in enumerate(bufs):
        L = buf.shape[1]
        src = jnp.concatenate([buf, jnp.stack([k[:, :, g], v[:, :, g]], axis=2)], axis=1)
        srcs.append(src)
        starts.append(PAST_LEN - L)
        states.append(src[:, -L:])
    return dswa_mix(q, srcs, q_pos, starts, t5_bias), states


def memory_kv(mem, g, w):
    B, N = mem.shape[:2]
    return (rms_norm(mem, g) @ w).reshape(B, N, 2, N_MEM_HEADS, HEAD_DIM)


def memory_attend(q, kv):
    s = jnp.einsum('bthd,bnhd->bhtn', q, kv[:, :, 0], preferred_element_type=jnp.float32) * (HEAD_DIM ** -0.5)
    p = jax.nn.softmax(s, axis=-1).astype(kv.dtype)
    return jnp.einsum('bhtn,bnhd->bthd', p, kv[:, :, 1])


def mla_kv_rows(x, pos, g_kv_in, w_kv_down, g_kv_latent):
    ckr = rms_norm(x, g_kv_in) @ w_kv_down
    c = rms_norm(ckr[..., :KV_LORA_RANK], g_kv_latent)
    kr = rope(ckr[..., None, KV_LORA_RANK:], pos)[..., 0, :]
    return jnp.concatenate([c, kr], axis=-1)


def mla_queries(q_down, pos, g_q_latent, w_q_up, w_uk):
    B, T = q_down.shape[:2]
    q = (rms_norm(q_down, g_q_latent) @ w_q_up).reshape(B, T, N_B_HEADS, QK_NOPE_DIM + QK_ROPE_DIM)
    q_lat = jnp.einsum('bthd,chd->bthc', q[..., :QK_NOPE_DIM], w_uk)
    q_rope = rope(q[..., QK_NOPE_DIM:], pos)
    return q_lat, q_rope


def mla_logits(q_lat, q_rope, rows):
    s = jnp.einsum('bthc,blc->bhtl', q_lat, rows[..., :KV_LORA_RANK], preferred_element_type=jnp.float32)
    s = s + jnp.einsum('bthr,blr->bhtl', q_rope, rows[..., KV_LORA_RANK:], preferred_element_type=jnp.float32)
    return s * MLA_SCALE


def mla_values(p, rows, w_uv):
    o_lat = jnp.einsum('bhtl,blc->bthc', p.astype(rows.dtype), rows[..., :KV_LORA_RANK])
    return jnp.einsum('bthc,chd->bthd', o_lat, w_uv)


def mla_prompt(q_lat, q_rope, rows, w_uv):
    B, S = q_lat.shape[:2]
    n_blk = S // Q_BLOCK
    ql = jnp.moveaxis(q_lat.reshape(B, n_blk, Q_BLOCK, N_B_HEADS, KV_LORA_RANK), 1, 0)
    qr = jnp.moveaxis(q_rope.reshape(B, n_blk, Q_BLOCK, N_B_HEADS, QK_ROPE_DIM), 1, 0)
    k_pos = jnp.arange(S, dtype=jnp.int32)

    def block(args):
        ql_b, qr_b, i = args
        q_pos = i * Q_BLOCK + jnp.arange(Q_BLOCK, dtype=jnp.int32)
        s = mla_logits(ql_b, qr_b, rows)
        s = jnp.where((k_pos[None, :] <= q_pos[:, None])[None, None], s, -jnp.inf)
        return mla_values(jax.nn.softmax(s, axis=-1), rows, w_uv)

    out = lax.map(block, (ql, qr, jnp.arange(n_blk, dtype=jnp.int32)))
    return jnp.moveaxis(out, 0, 1).reshape(B, S, N_B_HEADS, V_HEAD_DIM)


def mla_sample(q_lat, q_rope, new_rows, cache_mla_kv, page_table, w_uv):
    T = q_lat.shape[1]
    past = page_table.shape[1] * PAGE_SIZE
    t_idx = jnp.arange(T)
    mask = jnp.concatenate([jnp.ones((T, past), dtype=bool), t_idx[None, :] <= t_idx[:, None]], axis=1)

    def one(args):
        ql, qr, nr, pages = args
        rows = jnp.concatenate([cache_mla_kv[pages].reshape(past, MLA_ROW), nr], axis=0)[None]
        s = mla_logits(ql[None], qr[None], rows)
        s = jnp.where(mask[None, None], s, -jnp.inf)
        return mla_values(jax.nn.softmax(s, axis=-1), rows, w_uv)[0]

    return lax.map(one, (q_lat, q_rope, new_rows, page_table))


def trunk(x, pos, mem_kv, dswa_fn, mla_fn, p):
    B, T = x.shape[:2]
    swa_states = []
    kv_rows = None
    w_uk = p['w_kv_up'][..., :QK_NOPE_DIM]
    w_uv = p['w_kv_up'][..., QK_NOPE_DIM:]
    for l in range(DEPTH):
        h = rms_norm(x, p['g_attn_pre'][l])
        if l < N_A_LAYERS:
            z = h @ p['w_a_in'][l]
            qkv = z[..., :A_QKV_WIDTH].reshape(B, T, 3, N_GROUPS, HEADS_PER_GROUP, HEAD_DIM)
            o_tok, st = dswa_fn(l, qkv[:, :, 0], qkv[:, :, 1], qkv[:, :, 2])
            swa_states.append(st)
            z_mem = z[..., A_QKV_WIDTH:]
            w_out = p['w_a_out'][l]
        else:
            b = l - N_A_LAYERS
            if b == 0:
                kv_rows = mla_kv_rows(x, pos, p['g_kv_in'], p['w_kv_down'], p['g_kv_latent'])
            z = h @ p['w_b_in'][b]
            q_lat, q_rope = mla_queries(z[..., :Q_LORA_RANK], pos, p['g_q_latent'][b], p['w_q_up'][b], w_uk)
            o_tok = mla_fn(q_lat, q_rope, kv_rows, w_uv)
            z_mem = z[..., Q_LORA_RANK:]
            w_out = p['w_b_out'][b]
        o_mem = memory_attend(z_mem.reshape(B, T, N_MEM_HEADS, HEAD_DIM), mem_kv[l])
        o = jnp.concatenate([o_tok.reshape(B, T, -1), o_mem.reshape(B, T, -1)], axis=-1) @ w_out
        x = x + rms_norm(o, p['g_attn_post'][l])
        u = jnp.square(jax.nn.relu(rms_norm(x, p['g_mlp_pre'][l]) @ p['w_mlp_up'][l]))
        x = x + rms_norm(u @ p['w_mlp_down'][l], p['g_mlp_post'][l])
    swa_new = [jnp.stack([st[g] for st in swa_states]) for g in range(N_GROUPS)]
    return x, swa_new, kv_rows


def setup_inputs(seed: int = 0) -> dict:
    key = jax.random.key(seed)
    keys = iter(jax.random.split(key, 40))

    def normal(shape, scale):
        return jax.random.normal(next(keys), shape, jnp.float32) * scale

    def gain(shape):
        return 1.0 + 0.05 * jax.random.normal(next(keys), shape, jnp.float32)

    n_pages = PAST_LEN // PAGE_SIZE
    n_phys = (DEC_BATCH * n_pages * 5) // 4
    page_table = jax.random.permutation(next(keys), n_phys)[: DEC_BATCH * n_pages]
    page_table = page_table.reshape(DEC_BATCH, n_pages).astype(jnp.int32)
    swa = [normal((N_A_LAYERS, DEC_BATCH, min(w, PAST_LEN), 2, HEADS_PER_GROUP, HEAD_DIM), 1.0)
           for w, _ in DSWA_GROUPS]
    a_out_in = A_OUT_WIDTH + MEM_WIDTH
    b_out_in = N_B_HEADS * V_HEAD_DIM + MEM_WIDTH
    return {
        'x_prompt': normal((BATCH, SEQ, D_MODEL), 1.0),
        'x_sample': normal((DEC_BATCH, DEC_SEQ, D_MODEL), 1.0),
        'mem_prompt': normal((BATCH, N_MEM, D_MODEL), 1.0),
        'cache_swa_kv_w128': swa[0],
        'cache_swa_kv_w512': swa[1],
        'cache_swa_kv_w2048': swa[2],
        'cache_mla_kv': normal((n_phys, PAGE_SIZE, MLA_ROW), 1.0),
        'cache_mem_kv': normal((DEPTH, DEC_BATCH, N_MEM, 2, N_MEM_HEADS, HEAD_DIM), 1.0),
        'page_table': page_table,
        't5_bias': normal((N_BUCKETS, N_A_HEADS), 0.5),
        'g_attn_pre': gain((DEPTH, D_MODEL)),
        'g_attn_post': gain((DEPTH, D_MODEL)),
        'g_mlp_pre': gain((DEPTH, D_MODEL)),
        'g_mlp_post': gain((DEPTH, D_MODEL)),
        'g_mem': gain((DEPTH, D_MODEL)),
        'w_mem_kv': normal((DEPTH, D_MODEL, 2 * MEM_WIDTH), D_MODEL ** -0.5),
        'w_mlp_up': normal((DEPTH, D_MODEL, D_FF), D_MODEL ** -0.5),
        'w_mlp_down': normal((DEPTH, D_FF, D_MODEL), D_FF ** -0.5),
        'w_a_in': normal((N_A_LAYERS, D_MODEL, A_QKV_WIDTH + MEM_WIDTH), D_MODEL ** -0.5),
        'w_a_out': normal((N_A_LAYERS, a_out_in, D_MODEL), a_out_in ** -0.5),
        'g_kv_in': gain((D_MODEL,)),
        'w_kv_down': normal((D_MODEL, MLA_ROW), D_MODEL ** -0.5),
        'g_kv_latent': gain((KV_LORA_RANK,)),
        'w_kv_up': normal((KV_LORA_RANK, N_B_HEADS, QK_NOPE_DIM + V_HEAD_DIM), KV_LORA_RANK ** -0.5),
        'w_b_in': normal((N_B_LAYERS, D_MODEL, Q_LORA_RANK + MEM_WIDTH), D_MODEL ** -0.5),
        'g_q_latent': gain((N_B_LAYERS, Q_LORA_RANK)),
        'w_q_up': normal((N_B_LAYERS, Q_LORA_RANK, N_B_HEADS * (QK_NOPE_DIM + QK_ROPE_DIM)), Q_LORA_RANK ** -0.5),
        'w_b_out': normal((N_B_LAYERS, b_out_in, D_MODEL), b_out_in ** -0.5),
    }


def reference(x_prompt, x_sample, mem_prompt, cache_swa_kv_w128, cache_swa_kv_w512, cache_swa_kv_w2048,
              cache_mla_kv, cache_mem_kv, page_table, t5_bias, g_attn_pre, g_attn_post, g_mlp_pre, g_mlp_post,
              g_mem, w_mem_kv, w_mlp_up, w_mlp_down, w_a_in, w_a_out, g_kv_in, w_kv_down, g_kv_latent,
              w_kv_up, w_b_in, g_q_latent, w_q_up, w_b_out):
    p = {'g_attn_pre': g_attn_pre, 'g_attn_post': g_attn_post, 'g_mlp_pre': g_mlp_pre,
         'g_mlp_post': g_mlp_post, 'w_mlp_up': w_mlp_up, 'w_mlp_down': w_mlp_down,
         'w_a_in': w_a_in, 'w_a_out': w_a_out, 'g_kv_in': g_kv_in, 'w_kv_down': w_kv_down,
         'g_kv_latent': g_kv_latent, 'w_kv_up': w_kv_up, 'w_b_in': w_b_in,
         'g_q_latent': g_q_latent, 'w_q_up': w_q_up, 'w_b_out': w_b_out}
    pos_p = jnp.arange(x_prompt.shape[1], dtype=jnp.int32)
    pos_s = PAST_LEN + jnp.arange(x_sample.shape[1], dtype=jnp.int32)

    mem_kv_prompt = jnp.stack([memory_kv(mem_prompt, g_mem[l], w_mem_kv[l]) for l in range(DEPTH)])
    y_prompt, swa_p, mla_rows_prompt = trunk(
        x_prompt, pos_p, mem_kv_prompt,
        lambda a, q, k, v: dswa_prompt(q, k, v, t5_bias),
        mla_prompt, p)

    bufs = (cache_swa_kv_w128, cache_swa_kv_w512, cache_swa_kv_w2048)
    y_sample, swa_s, mla_rows_sample = trunk(
        x_sample, pos_s, cache_mem_kv,
        lambda a, q, k, v: dswa_sample(q, k, v, [b[a] for b in bufs], pos_s, t5_bias),
        lambda ql, qr, rows, w_uv: mla_sample(ql, qr, rows, cache_mla_kv, page_table, w_uv), p)

    return (y_prompt, y_sample, swa_p[0], swa_p[1], swa_p[2], mla_rows_prompt, mem_kv_prompt,
            swa_s[0], swa_s[1], swa_s[2], mla_rows_sample)
```

```python
import functools
import math

import jax
import jax.numpy as jnp
from jax import lax
from jax.experimental import pallas as pl
from jax.experimental.pallas import tpu as pltpu

F32 = jnp.float32
BF16 = jnp.bfloat16

HEAD_DIM = 128
DSWA_GROUPS = ((128, 1), (512, 4), (2048, 16))
N_GROUPS = len(DSWA_GROUPS)
HEADS_PER_GROUP = 4
GROUP_WIDTH = HEADS_PER_GROUP * HEAD_DIM
N_A_HEADS = N_GROUPS * HEADS_PER_GROUP
A_QKV_WIDTH = 3 * N_A_HEADS * HEAD_DIM
N_MEM_HEADS = 4
MEM_WIDTH = N_MEM_HEADS * HEAD_DIM
N_B_HEADS = 12
Q_LORA_RANK = 1536
KV_LORA_RANK = 512
QK_NOPE_DIM = 128
QK_ROPE_DIM = 64
V_HEAD_DIM = 128
MLA_ROW = KV_LORA_RANK + QK_ROPE_DIM
MLA_SCALE = (QK_NOPE_DIM + QK_ROPE_DIM) ** -0.5
ATTN_SCALE = HEAD_DIM ** -0.5
ROPE_THETA = 10000.0
N_BUCKETS = 32
MAX_DISTANCE = 2048
EPS = 1e-6
PAGE_SIZE = 128

LANES = 128
ROPE_PAD = LANES
Q_HEAD_PAD = QK_NOPE_DIM + ROPE_PAD
MLA_ROW_PAD = KV_LORA_RANK + ROPE_PAD
NEG = -1e30
VMEM_BYTES_V7X = 64 * 1024 * 1024
VMEM_CAP = VMEM_BYTES_V7X - 8 * 1024 * 1024
PAGES_PER_STEP = 8


def _tile(n, pref):
    return pref if n % pref == 0 else n


def _nbytes(shape, dtype):
    return math.prod(shape) * jnp.dtype(dtype).itemsize


def _params(semantics, pipelined_bytes, resident_bytes=0):
    est = 2 * pipelined_bytes + 2 * resident_bytes + (4 << 20)
    return pltpu.CompilerParams(dimension_semantics=semantics,
                                vmem_limit_bytes=int(min(max(est, 16 << 20), VMEM_CAP)))


def _rms(x, g):
    return x * lax.rsqrt(jnp.mean(x * x, axis=-1, keepdims=True) + EPS) * g


def _rope_pad(r, cc, shi, slo):
    half = QK_ROPE_DIM // 2
    return r * cc + pltpu.roll(r, half, axis=1) * shi + pltpu.roll(r, LANES - half, axis=1) * slo


def _dot(a, b):
    return jnp.dot(a, b, preferred_element_type=F32)


def _dot_nt(a, b):
    return lax.dot_general(a, b, (((1,), (1,)), ((), ())), preferred_element_type=F32)


def _norm_matmul_kernel(*refs, norm, rope, scale):
    if rope:
        x_ref, g_ref, w_ref, cc_ref, shi_ref, slo_ref, o_ref, xn_ref = refs
    else:
        x_ref, g_ref, w_ref, o_ref, xn_ref = refs

    @pl.when(pl.program_id(1) == 0)
    def _():
        x = x_ref[...].astype(F32)
        if norm:
            x = _rms(x, g_ref[...])
        xn_ref[...] = x.astype(BF16)

    acc = _dot(xn_ref[...], w_ref[...])
    if scale != 1.0:
        acc = acc * scale
    if rope:
        cc, shi, slo = cc_ref[...], shi_ref[...], slo_ref[...]
        for h in range(acc.shape[1] // Q_HEAD_PAD):
            lo = h * Q_HEAD_PAD
            o_ref[:, lo:lo + QK_NOPE_DIM] = acc[:, lo:lo + QK_NOPE_DIM].astype(o_ref.dtype)
            r = _rope_pad(acc[:, lo + QK_NOPE_DIM:lo + Q_HEAD_PAD], cc, shi, slo)
            o_ref[:, lo + QK_NOPE_DIM:lo + Q_HEAD_PAD] = r.astype(o_ref.dtype)
    else:
        o_ref[...] = acc.astype(o_ref.dtype)


def norm_matmul(x, g, w, *, k_width=None, norm=True, rope_tables=None, scale=1.0,
                out_dtype=F32, tm=512, tn=512):
    n = x.shape[0]
    k = k_width or x.shape[1]
    nout = w.shape[1]
    tm = _tile(n, tm)
    tn = _tile(nout, tn)
    rope = rope_tables is not None
    in_specs = [pl.BlockSpec((tm, k), lambda i, j: (i, 0)),
                pl.BlockSpec((1, k), lambda i, j: (0, 0)),
                pl.BlockSpec((k, tn), lambda i, j: (0, j))]
    args = [x, g.reshape(1, k).astype(F32), w]
    if rope:
        in_specs += [pl.BlockSpec((tm, LANES), lambda i, j: (i, 0))] * 3
        args += list(rope_tables)
    blocks = (_nbytes((tm, k), x.dtype) + _nbytes((k, tn), w.dtype) + _nbytes((tm, tn), out_dtype)
              + 3 * _nbytes((tm, LANES), F32))
    return pl.pallas_call(
        functools.partial(_norm_matmul_kernel, norm=norm, rope=rope, scale=scale),
        grid=(n // tm, nout // tn),
        in_specs=in_specs,
        out_specs=pl.BlockSpec((tm, tn), lambda i, j: (i, j)),
        out_shape=jax.ShapeDtypeStruct((n, nout), out_dtype),
        scratch_shapes=[pltpu.VMEM((tm, k), BF16)],
        compiler_params=_params(("parallel", "arbitrary"), blocks,
                                _nbytes((tm, k), F32) + _nbytes((tm, tn), F32)),
        name="norm_matmul",
    )(*args)


def _kv_finish_kernel(raw_ref, g_ref, cc_ref, shi_ref, slo_ref, rows_ref, c_ref, kr_ref):
    c = _rms(raw_ref[:, :KV_LORA_RANK], g_ref[...])
    kr = _rope_pad(raw_ref[:, KV_LORA_RANK:], cc_ref[...], shi_ref[...], slo_ref[...])
    rows_ref[:, :KV_LORA_RANK] = c
    rows_ref[:, KV_LORA_RANK:] = kr
    c_ref[...] = c.astype(BF16)
    kr_ref[...] = kr.astype(BF16)


def kv_finish(raw, g_latent, rope_tables, *, tm=512):
    n = raw.shape[0]
    tm = _tile(n, tm)
    row = lambda w: pl.BlockSpec((tm, w), lambda i: (i, 0))
    blocks = 2 * _nbytes((tm, MLA_ROW_PAD), F32) + 4 * _nbytes((tm, LANES), F32) + _nbytes((tm, KV_LORA_RANK), BF16)
    return pl.pallas_call(
        _kv_finish_kernel,
        grid=(n // tm,),
        in_specs=[row(MLA_ROW_PAD), pl.BlockSpec((1, KV_LORA_RANK), lambda i: (0, 0)),
                  row(LANES), row(LANES), row(LANES)],
        out_specs=[row(MLA_ROW_PAD), row(KV_LORA_RANK), row(LANES)],
        out_shape=[jax.ShapeDtypeStruct((n, MLA_ROW_PAD), F32),
                   jax.ShapeDtypeStruct((n, KV_LORA_RANK), BF16),
                   jax.ShapeDtypeStruct((n, LANES), BF16)],
        compiler_params=_params(("parallel",), blocks),
        name="kv_finish",
    )(raw, g_latent.reshape(1, -1).astype(F32), *rope_tables)


def _out_proj_kernel(a1_ref, a2_ref, w1_ref, w2_ref, g_ref, x_ref, o_ref):
    o = _dot(a1_ref[...].astype(BF16), w1_ref[...]) + _dot(a2_ref[...].astype(BF16), w2_ref[...])
    o_ref[...] = x_ref[...] + _rms(o, g_ref[...])


def out_proj(a1, a2, w1, w2, g, x, *, tm=512):
    n, d = x.shape
    k1, k2 = a1.shape[1], a2.shape[1]
    tm = _tile(n, tm)
    blocks = (_nbytes((tm, k1), a1.dtype) + _nbytes((tm, k2), a2.dtype) + _nbytes((k1 + k2, d), BF16)
              + 2 * _nbytes((tm, d), F32))
    return pl.pallas_call(
        _out_proj_kernel,
        grid=(n // tm,),
        in_specs=[pl.BlockSpec((tm, k1), lambda i: (i, 0)), pl.BlockSpec((tm, k2), lambda i: (i, 0)),
                  pl.BlockSpec((k1, d), lambda i: (0, 0)), pl.BlockSpec((k2, d), lambda i: (0, 0)),
                  pl.BlockSpec((1, d), lambda i: (0, 0)), pl.BlockSpec((tm, d), lambda i: (i, 0))],
        out_specs=pl.BlockSpec((tm, d), lambda i: (i, 0)),
        out_shape=jax.ShapeDtypeStruct((n, d), F32),
        compiler_params=_params(("parallel",), blocks, _nbytes((tm, d), F32)),
        name="out_proj",
    )(a1, a2, w1, w2, g.reshape(1, d).astype(F32), x)


def _mlp_kernel(x_ref, gpre_ref, wup_ref, wdn_ref, gpost_ref, o_ref, xn_ref, acc_ref):
    j = pl.program_id(1)

    @pl.when(j == 0)
    def _():
        xn_ref[...] = _rms(x_ref[...], gpre_ref[...]).astype(BF16)

    h = _dot(xn_ref[...], wup_ref[...])
    u = jnp.square(jnp.maximum(h, 0.0)).astype(BF16)
    part = _dot(u, wdn_ref[...])

    @pl.when(j == 0)
    def _():
        acc_ref[...] = part

    @pl.when(j > 0)
    def _():
        acc_ref[...] += part

    @pl.when(j == pl.num_programs(1) - 1)
    def _():
        o_ref[...] = x_ref[...] + _rms(acc_ref[...], gpost_ref[...])


def mlp(x, g_pre, w_up, w_down, g_post, *, tm=512, tf=1024):
    n, d = x.shape
    f = w_up.shape[1]
    tm = _tile(n, tm)
    tf = _tile(f, tf)
    blocks = 2 * _nbytes((tm, d), F32) + 2 * _nbytes((d, tf), BF16)
    resident = _nbytes((tm, d), BF16) + _nbytes((tm, d), F32) + _nbytes((tm, tf), F32)
    return pl.pallas_call(
        _mlp_kernel,
        grid=(n // tm, f // tf),
        in_specs=[pl.BlockSpec((tm, d), lambda i, j: (i, 0)), pl.BlockSpec((1, d), lambda i, j: (0, 0)),
                  pl.BlockSpec((d, tf), lambda i, j: (0, j)), pl.BlockSpec((tf, d), lambda i, j: (j, 0)),
                  pl.BlockSpec((1, d), lambda i, j: (0, 0))],
        out_specs=pl.BlockSpec((tm, d), lambda i, j: (i, 0)),
        out_shape=jax.ShapeDtypeStruct((n, d), F32),
        scratch_shapes=[pltpu.VMEM((tm, d), BF16), pltpu.VMEM((tm, d), F32)],
        compiler_params=_params(("parallel", "arbitrary"), blocks, resident),
        name="mlp",
    )(x, g_pre.reshape(1, d).astype(F32), w_up, w_down, g_post.reshape(1, d).astype(F32))


def _head_matmul_kernel(x_ref, w_ref, o_ref):
    o_ref[...] = _dot(x_ref[...].astype(BF16), w_ref[0]).astype(o_ref.dtype)


def head_matmul(x, w, *, x_block_stride=1, out_dtype=F32):
    n = x.shape[0]
    nh, kh, nn = w.shape
    blocks = _nbytes((n, kh), x.dtype) + _nbytes((kh, nn), w.dtype) + _nbytes((n, nn), out_dtype)
    return pl.pallas_call(
        _head_matmul_kernel,
        grid=(nh,),
        in_specs=[pl.BlockSpec((n, kh), lambda h: (0, h * x_block_stride)),
                  pl.BlockSpec((1, kh, nn), lambda h: (h, 0, 0))],
        out_specs=pl.BlockSpec((n, nn), lambda h: (0, h)),
        out_shape=jax.ShapeDtypeStruct((n, nh * nn), out_dtype),
        compiler_params=_params(("parallel",), blocks),
        name="head_matmul",
    )(x, w)


def _combine_kernel(o0, o1, o2, l0, l1, l2, out_ref):
    m = jnp.maximum(jnp.maximum(l0[...], l1[...]), l2[...])
    w0, w1, w2 = jnp.exp(l0[...] - m), jnp.exp(l1[...] - m), jnp.exp(l2[...] - m)
    out_ref[...] = (w0 * o0[...] + w1 * o1[...] + w2 * o2[...]) / (w0 + w1 + w2)


def combine_groups(outs, lses, *, tm=512):
    n, w = outs[0].shape
    tm = _tile(n, tm)
    spec = pl.BlockSpec((tm, w), lambda i: (i, 0))
    return pl.pallas_call(
        _combine_kernel,
        grid=(n // tm,),
        in_specs=[spec] * 6,
        out_specs=spec,
        out_shape=jax.ShapeDtypeStruct((n, w), F32),
        compiler_params=_params(("parallel",), 7 * _nbytes((tm, w), F32)),
        name="combine_groups",
    )(*outs, *lses)


def _dswa_prompt_kernel(q_ref, kp_ref, kc_ref, vp_ref, vc_ref, bias_ref, o_ref, lse_ref):
    blk = q_ref.shape[0]
    col = lax.broadcasted_iota(jnp.int32, (blk, 2 * blk), 1)
    prev_invalid = col < jnp.where(pl.program_id(1) == 0, blk, 0)
    for h in range(HEADS_PER_GROUP):
        sl = slice(h * HEAD_DIM, (h + 1) * HEAD_DIM)
        q = q_ref[:, sl].astype(BF16)
        k = jnp.concatenate([kp_ref[:, sl], kc_ref[:, sl]], axis=0).astype(BF16)
        v = jnp.concatenate([vp_ref[:, sl], vc_ref[:, sl]], axis=0).astype(BF16)
        s = _dot_nt(q, k) * ATTN_SCALE + bias_ref[h]
        s = jnp.where(prev_invalid, NEG, s)
        m = jnp.max(s, axis=-1, keepdims=True)
        e = jnp.exp(s - m)
        l = jnp.sum(e, axis=-1, keepdims=True)
        acc = _dot(e.astype(BF16), v)
        o_ref[:, sl] = acc / l
        lse_ref[:, sl] = jnp.broadcast_to(m + jnp.log(l), (blk, HEAD_DIM))


def dswa_prompt(z, g, bias):
    s_len, zw = z.shape
    w, d = DSWA_GROUPS[g]
    blk = w // d
    sub = s_len // d
    nblk = sub // blk
    zv = z.reshape(sub, d * zw)
    per_res = zw // GROUP_WIDTH
    q_col, k_col, v_col = g, N_GROUPS + g, 2 * N_GROUPS + g
    cur = lambda c: pl.BlockSpec((blk, GROUP_WIDTH), lambda r, u: (u, r * per_res + c))
    prev = lambda c: pl.BlockSpec((blk, GROUP_WIDTH), lambda r, u: (jnp.maximum(u - 1, 0), r * per_res + c))
    out_spec = pl.BlockSpec((blk, GROUP_WIDTH), lambda r, u: (u, r))
    blocks = 7 * _nbytes((blk, GROUP_WIDTH), F32) + _nbytes(bias.shape, F32)
    o, lse = pl.pallas_call(
        _dswa_prompt_kernel,
        grid=(d, nblk),
        in_specs=[cur(q_col), prev(k_col), cur(k_col), prev(v_col), cur(v_col),
                  pl.BlockSpec(bias.shape, lambda r, u: (0, 0, 0))],
        out_specs=[out_spec, out_spec],
        out_shape=[jax.ShapeDtypeStruct((sub, d * GROUP_WIDTH), F32)] * 2,
        compiler_params=_params(("parallel", "arbitrary"), blocks),
        name=f"dswa_prompt_g{g}",
    )(zv, zv, zv, zv, zv, bias)
    return o.reshape(s_len, GROUP_WIDTH), lse.reshape(s_len, GROUP_WIDTH)


def _mem_attn_prompt_kernel(q_ref, kv_ref, o_ref):
    for h in range(N_MEM_HEADS):
        sl = slice(h * HEAD_DIM, (h + 1) * HEAD_DIM)
        q = q_ref[:, sl].astype(BF16)
        k = kv_ref[:, sl].astype(BF16)
        v = kv_ref[:, MEM_WIDTH + h * HEAD_DIM:MEM_WIDTH + (h + 1) * HEAD_DIM].astype(BF16)
        s = _dot_nt(q, k) * ATTN_SCALE
        m = jnp.max(s, axis=-1, keepdims=True)
        e = jnp.exp(s - m)
        l = jnp.sum(e, axis=-1, keepdims=True)
        o_ref[:, sl] = _dot(e.astype(BF16), v) / l


def mem_attn_prompt(z, col_block, mem_kv, *, tm=512):
    n = z.shape[0]
    tm = _tile(n, tm)
    blocks = 2 * _nbytes((tm, MEM_WIDTH), F32) + _nbytes(mem_kv.shape, F32)
    return pl.pallas_call(
        _mem_attn_prompt_kernel,
        grid=(n // tm,),
        in_specs=[pl.BlockSpec((tm, MEM_WIDTH), lambda i: (i, col_block)),
                  pl.BlockSpec(mem_kv.shape, lambda i: (0, 0))],
        out_specs=pl.BlockSpec((tm, MEM_WIDTH), lambda i: (i, 0)),
        out_shape=jax.ShapeDtypeStruct((n, MEM_WIDTH), F32),
        compiler_params=_params(("parallel",), blocks, _nbytes((tm, mem_kv.shape[0]), F32) * 4),
        name="mem_attn_prompt",
    )(z, mem_kv)


def _mla_prompt_kernel(q_ref, kn_ref, v_ref, kr_ref, o_ref, kcat_ref, *, tq):
    qi = pl.program_id(1)

    @pl.when(qi == 0)
    def _():
        kcat_ref[:, :QK_NOPE_DIM] = kn_ref[...]
        kcat_ref[:, QK_NOPE_DIM:] = kr_ref[...]

    q = q_ref[...]

    def step(kb, carry, masked):
        m, l, acc = carry
        start = pl.multiple_of(kb * tq, tq)
        k = kcat_ref[pl.ds(start, tq), :]
        v = v_ref[pl.ds(start, tq), :]
        s = _dot_nt(q, k)
        if masked:
            row = lax.broadcasted_iota(jnp.int32, (tq, tq), 0)
            col = lax.broadcasted_iota(jnp.int32, (tq, tq), 1)
            s = jnp.where(col <= row, s, NEG)
        m_new = jnp.maximum(m, jnp.max(s, axis=-1, keepdims=True))
        alpha = jnp.exp(m - m_new)
        p = jnp.exp(s - m_new)
        l = alpha * l + jnp.sum(p, axis=-1, keepdims=True)
        acc = alpha * acc + _dot(p.astype(BF16), v)
        return m_new, l, acc

    init = (jnp.full((tq, 1), NEG, F32), jnp.zeros((tq, 1), F32), jnp.zeros((tq, V_HEAD_DIM), F32))
    carry = lax.fori_loop(0, qi, lambda kb, c: step(kb, c, False), init)
    m, l, acc = step(qi, carry, True)
    o_ref[...] = (acc / l).astype(o_ref.dtype)


def mla_prompt(q, kv, kr, *, tq=512):
    s_len = q.shape[0]
    tq = _tile(s_len, tq)
    nh = N_B_HEADS
    blocks = (_nbytes((tq, Q_HEAD_PAD), BF16) + 3 * _nbytes((s_len, LANES), BF16) + _nbytes((tq, V_HEAD_DIM), BF16))
    resident = _nbytes((s_len, Q_HEAD_PAD), BF16) + 4 * _nbytes((tq, tq), F32)
    return pl.pallas_call(
        functools.partial(_mla_prompt_kernel, tq=tq),
        grid=(nh, s_len // tq),
        in_specs=[pl.BlockSpec((tq, Q_HEAD_PAD), lambda h, i: (i, h)),
                  pl.BlockSpec((s_len, QK_NOPE_DIM), lambda h, i: (0, h)),
                  pl.BlockSpec((s_len, V_HEAD_DIM), lambda h, i: (0, nh + h)),
                  pl.BlockSpec((s_len, ROPE_PAD), lambda h, i: (0, 0))],
        out_specs=pl.BlockSpec((tq, V_HEAD_DIM), lambda h, i: (i, h)),
        out_shape=jax.ShapeDtypeStruct((s_len, nh * V_HEAD_DIM), BF16),
        scratch_shapes=[pltpu.VMEM((s_len, Q_HEAD_PAD), BF16)],
        compiler_params=_params(("parallel", "arbitrary"), blocks, resident),
        name="mla_prompt",
    )(q, kv, kv, kr)


def _row_scores(q, k, bias):
    return jnp.sum(k * q[None], axis=-1, keepdims=True) * ATTN_SCALE + bias


def _dswa_sample_kernel(buf_ref, new_ref, q_ref, bias_a_ref, bias_b_ref, nbuf_ref, o_ref, lse_ref, *, d):
    nsub = buf_ref.shape[1]
    nh = HEADS_PER_GROUP
    t_new = new_ref.shape[1]
    for r_out in range(d):
        sh, r_in = divmod(r_out + t_new, d)
        if nsub - sh > 0:
            nbuf_ref[0, 0:nsub - sh, r_out] = buf_ref[0, sh:nsub, r_in]
        for a in range(max(nsub - sh, 0), nsub):
            nbuf_ref[0, a, r_out] = new_ref[0, a * d + r_out + t_new - nsub * d]
    k_new = new_ref[0, :, 0:nh, :]
    v_new = new_ref[0, :, nh:2 * nh, :]
    for i in range(t_new):
        q = q_ref[0, i]
        k_old = buf_ref[0, :, i % d, 0:nh, :]
        v_old = buf_ref[0, :, i % d, nh:2 * nh, :]
        s_a = _row_scores(q, k_old, bias_a_ref[i])
        s_b = _row_scores(q, k_new, bias_b_ref[i])
        m = jnp.maximum(jnp.max(s_a, axis=0), jnp.max(s_b, axis=0))
        e_a = jnp.exp(s_a - m[None])
        e_b = jnp.exp(s_b - m[None])
        l = jnp.sum(e_a, axis=0) + jnp.sum(e_b, axis=0)
        acc = jnp.sum(e_a * v_old, axis=0) + jnp.sum(e_b * v_new, axis=0)
        o_ref[0, i] = acc / l
        lse_ref[0, i] = m + jnp.log(l)


def dswa_sample(buf, new_kv, q, bias_a, bias_b, d):
    b, l_buf = buf.shape[:2]
    t_new = new_kv.shape[1]
    nsub = l_buf // d
    rows = 2 * HEADS_PER_GROUP
    bufv = buf.reshape(b, nsub, d, rows, HEAD_DIM)
    buf_spec = pl.BlockSpec((1, nsub, d, rows, HEAD_DIM), lambda i: (i, 0, 0, 0, 0))
    tok = lambda r: pl.BlockSpec((1, t_new, r, HEAD_DIM), lambda i: (i, 0, 0, 0))
    full = lambda a: pl.BlockSpec(a.shape, lambda i: (0,) * a.ndim)
    blocks = 2 * _nbytes((l_buf, rows, HEAD_DIM), F32) + _nbytes(bias_a.shape, F32)
    nbuf, o, lse = pl.pallas_call(
        functools.partial(_dswa_sample_kernel, d=d),
        grid=(b,),
        in_specs=[buf_spec, tok(rows), tok(HEADS_PER_GROUP), full(bias_a), full(bias_b)],
        out_specs=[buf_spec, tok(HEADS_PER_GROUP), tok(HEADS_PER_GROUP)],
        out_shape=[jax.ShapeDtypeStruct(bufv.shape, F32),
                   jax.ShapeDtypeStruct((b, t_new, HEADS_PER_GROUP, HEAD_DIM), F32),
                   jax.ShapeDtypeStruct((b, t_new, HEADS_PER_GROUP, HEAD_DIM), F32)],
        compiler_params=_params(("parallel",), blocks),
        name=f"dswa_sample_d{d}",
    )(bufv, new_kv, q, bias_a, bias_b)
    return nbuf.reshape(buf.shape), o, lse


def _mem_attn_sample_kernel(kv_ref, q_ref, o_ref):
    nh = N_MEM_HEADS
    k = kv_ref[0, 0, :, 0:nh, :]
    v = kv_ref[0, 0, :, nh:2 * nh, :]
    for i in range(q_ref.shape[1]):
        s = _row_scores(q_ref[0, i], k, jnp.zeros((1, 1, HEAD_DIM), F32))
        m = jnp.max(s, axis=0)
        e = jnp.exp(s - m[None])
        o_ref[0, i] = jnp.sum(e * v, axis=0) / jnp.sum(e, axis=0)


def mem_attn_sample(cache, layer, q):
    _, b, n_mem, rows, _ = cache.shape
    t_new = q.shape[1]
    tok = pl.BlockSpec((1, t_new, N_MEM_HEADS, HEAD_DIM), lambda i: (i, 0, 0, 0))
    return pl.pallas_call(
        _mem_attn_sample_kernel,
        grid=(b,),
        in_specs=[pl.BlockSpec((1, 1, n_mem, rows, HEAD_DIM), lambda i: (layer, i, 0, 0, 0)), tok],
        out_specs=tok,
        out_shape=jax.ShapeDtypeStruct(q.shape, F32),
        compiler_params=_params(("parallel",), _nbytes((n_mem, rows, HEAD_DIM), F32),
                                4 * _nbytes((n_mem, rows, HEAD_DIM), F32)),
        name="mem_attn_sample",
    )(cache, q)


def _mla_sample_kernel(pt_ref, q_ref, new_ref, *rest, n_pages, t_new):
    del pt_ref
    page_refs = rest[:n_pages]
    o_ref, m_ref, l_ref, acc_ref = rest[n_pages:]
    c = pl.program_id(1)
    nq = q_ref.shape[1]

    @pl.when(c == 0)
    def _():
        m_ref[...] = jnp.full(m_ref.shape, NEG, F32)
        l_ref[...] = jnp.zeros(l_ref.shape, F32)
        acc_ref[...] = jnp.zeros(acc_ref.shape, F32)

    q = q_ref[0]

    def update(s_parts, v_parts):
        m_prev = m_ref[...]
        m_cur = functools.reduce(jnp.maximum, [jnp.max(s, axis=-1, keepdims=True) for s in s_parts])
        m_new = jnp.maximum(m_prev, m_cur)
        alpha = jnp.exp(m_prev - m_new)
        l_new = alpha * l_ref[...]
        acc = alpha * acc_ref[...]
        for s, v in zip(s_parts, v_parts):
            p = jnp.exp(s - m_new)
            l_new = l_new + jnp.sum(p, axis=-1, keepdims=True)
            acc = acc + _dot(p.astype(BF16), v)
        m_ref[...] = m_new
        l_ref[...] = l_new
        acc_ref[...] = acc

    rows = [r[0].astype(BF16) for r in page_refs]
    update([_dot_nt(q, r) for r in rows], [r[:, :KV_LORA_RANK] for r in rows])

    @pl.when(c == pl.num_programs(1) - 1)
    def _():
        nr = new_ref[0].astype(BF16)
        s = _dot_nt(q, nr)
        tok = lax.broadcasted_iota(jnp.int32, (nq, t_new), 0) // (nq // t_new)
        key = lax.broadcasted_iota(jnp.int32, (nq, t_new), 1)
        update([jnp.where(key <= tok, s, NEG)], [nr[:, :KV_LORA_RANK]])
        o_ref[0] = acc_ref[...] / l_ref[...]


def mla_sample(q, new_rows, cache, page_table):
    b, nq, row = q.shape
    t_new = new_rows.shape[1]
    n_seq_pages = page_table.shape[1]
    pps = PAGES_PER_STEP if n_seq_pages % PAGES_PER_STEP == 0 else 1
    page_spec = lambda p: pl.BlockSpec((1, PAGE_SIZE, row), lambda i, c, pt: (pt[i, c * pps + p], 0, 0))
    blocks = pps * _nbytes((PAGE_SIZE, row), F32) + _nbytes((nq, row), BF16) + _nbytes((nq, KV_LORA_RANK), F32)
    return pl.pallas_call(
        functools.partial(_mla_sample_kernel, n_pages=pps, t_new=t_new),
        grid_spec=pltpu.PrefetchScalarGridSpec(
            num_scalar_prefetch=1,
            grid=(b, n_seq_pages // pps),
            in_specs=[pl.BlockSpec((1, nq, row), lambda i, c, pt: (i, 0, 0)),
                      pl.BlockSpec((1, t_new, row), lambda i, c, pt: (i, 0, 0))]
                     + [page_spec(p) for p in range(pps)],
            out_specs=pl.BlockSpec((1, nq, KV_LORA_RANK), lambda i, c, pt: (i, 0, 0)),
            scratch_shapes=[pltpu.VMEM((nq, 1), F32), pltpu.VMEM((nq, 1), F32),
                            pltpu.VMEM((nq, KV_LORA_RANK), F32)]),
        out_shape=jax.ShapeDtypeStruct((b, nq, KV_LORA_RANK), F32),
        compiler_params=_params(("parallel", "arbitrary"), blocks,
                                pps * _nbytes((PAGE_SIZE, row), F32)),
        name="mla_sample",
    )(page_table, q, new_rows, *([cache] * pps))


def _t5_bucket(dist):
    max_exact = N_BUCKETS // 2
    dd = jnp.maximum(dist, 1).astype(F32)
    large = max_exact + (jnp.log(dd / max_exact) / math.log(MAX_DISTANCE / max_exact)
                         * (N_BUCKETS - max_exact)).astype(jnp.int32)
    large = jnp.minimum(large, N_BUCKETS - 1)
    return jnp.where(dist < max_exact, dist, large)


def _group_bias(t5_bias, g):
    w, d = DSWA_GROUPS[g]
    offs = jnp.arange(w // d + 1, dtype=jnp.int32) * d
    return t5_bias[_t5_bucket(offs), g * HEADS_PER_GROUP:(g + 1) * HEADS_PER_GROUP].astype(F32)


def _banded(table, j):
    jmax = table.shape[0] - 1
    vals = table[jnp.clip(j, 0, jmax)]
    return jnp.where(((j >= 0) & (j <= jmax))[..., None], vals, NEG)


def _prompt_bias(table):
    blk = table.shape[0] - 1
    j = jnp.arange(blk)[:, None] + blk - jnp.arange(2 * blk)[None, :]
    return jnp.moveaxis(_banded(table, j), -1, 0)


def _sample_bias(table, d, t_new):
    blk = table.shape[0] - 1
    i = jnp.arange(t_new)
    j_a = blk - jnp.arange(blk)[None, :] + (i // d)[:, None]
    diff = i[:, None] - i[None, :]
    j_b = jnp.where((diff >= 0) & (diff % d == 0), diff // d, -1)
    lanes = lambda x: jnp.broadcast_to(x[..., None], x.shape + (HEAD_DIM,))
    return lanes(_banded(table, j_a)), lanes(_banded(table, j_b))


def _rope_tables(pos):
    half = QK_ROPE_DIM // 2
    inv_freq = ROPE_THETA ** (-jnp.arange(half, dtype=F32) / half)
    ang = pos.astype(F32)[:, None] * inv_freq[None, :]
    cos, sin = jnp.cos(ang), jnp.sin(ang)
    zero = jnp.zeros_like(cos)
    cc = jnp.concatenate([cos, cos, zero, zero], axis=-1)
    shi = jnp.concatenate([zero, sin, zero, zero], axis=-1)
    slo = jnp.concatenate([-sin, zero, zero, zero], axis=-1)
    return cc, shi, slo


def _pad_last(w, to):
    return jnp.pad(w, [(0, 0)] * (w.ndim - 1) + [(0, to - w.shape[-1])])


def _trunk(x, rope_tables, dswa_fn, mem_fn, mla_fn, p):
    n = x.shape[0]
    z0 = norm_matmul(x, p['g_attn_pre'][0], p['w_a_in'])
    o_tok = dswa_fn(z0)
    o_mem = mem_fn(0, z0, A_QKV_WIDTH // MEM_WIDTH)
    x = out_proj(o_tok, o_mem, p['w_a_out'][:GROUP_WIDTH], p['w_a_out'][GROUP_WIDTH:], p['g_attn_post'][0], x)
    x = mlp(x, p['g_mlp_pre'][0], p['w_mlp_up'][0], p['w_mlp_down'][0], p['g_mlp_post'][0])
    raw = norm_matmul(x, p['g_kv_in'], p['w_kv_down'])
    rows, c_bf, kr_bf = kv_finish(raw, p['g_kv_latent'], rope_tables)
    z = norm_matmul(x, p['g_attn_pre'][1], p['w_b_in'])
    q = norm_matmul(z, p['g_q_latent'], p['w_q_up'], k_width=Q_LORA_RANK, rope_tables=rope_tables,
                    scale=MLA_SCALE, out_dtype=BF16, tn=2 * Q_HEAD_PAD)
    o_tok = mla_fn(q, rows, c_bf, kr_bf)
    o_mem = mem_fn(1, z, Q_LORA_RANK // MEM_WIDTH)
    nv = N_B_HEADS * V_HEAD_DIM
    x = out_proj(o_tok, o_mem, p['w_b_out'][:nv], p['w_b_out'][nv:], p['g_attn_post'][1], x)
    x = mlp(x, p['g_mlp_pre'][1], p['w_mlp_up'][1], p['w_mlp_down'][1], p['g_mlp_post'][1])
    return x, rows[:, :MLA_ROW], z0


def kernel(x_prompt, x_sample, mem_prompt, cache_swa_kv_w128, cache_swa_kv_w512, cache_swa_kv_w2048, cache_mla_kv, cache_mem_kv, page_table, t5_bias, g_attn_pre, g_attn_post, g_mlp_pre, g_mlp_post, g_mem, w_mem_kv, w_mlp_up, w_mlp_down, w_a_in, w_a_out, g_kv_in, w_kv_down, g_kv_latent, w_kv_up, w_b_in, g_q_latent, w_q_up, w_b_out):
    depth = g_attn_pre.shape[0]
    assert depth == 2 and w_a_in.shape[0] == 1 and w_b_in.shape[0] == 1
    bp, s_len, d_model = x_prompt.shape
    assert bp == 1
    bs, t_new, _ = x_sample.shape
    past = page_table.shape[1] * PAGE_SIZE
    bufs = (cache_swa_kv_w128, cache_swa_kv_w512, cache_swa_kv_w2048)
    for buf, (w, d) in zip(bufs, DSWA_GROUPS):
        assert buf.shape[2] == w and past >= w and s_len % (d * (w // d)) == 0 and s_len >= w

    w_q_up_pad = _pad_last(w_q_up[0].reshape(Q_LORA_RANK, N_B_HEADS, QK_NOPE_DIM + QK_ROPE_DIM),
                           Q_HEAD_PAD).reshape(Q_LORA_RANK, N_B_HEADS * Q_HEAD_PAD)
    p = {
        'g_attn_pre': g_attn_pre, 'g_attn_post': g_attn_post, 'g_mlp_pre': g_mlp_pre, 'g_mlp_post': g_mlp_post,
        'g_kv_in': g_kv_in, 'g_kv_latent': g_kv_latent, 'g_q_latent': g_q_latent[0],
        'w_a_in': w_a_in[0].astype(BF16), 'w_a_out': w_a_out[0].astype(BF16),
        'w_b_in': w_b_in[0].astype(BF16), 'w_b_out': w_b_out[0].astype(BF16),
        'w_mlp_up': w_mlp_up.astype(BF16), 'w_mlp_down': w_mlp_down.astype(BF16),
        'w_kv_down': _pad_last(w_kv_down, MLA_ROW_PAD).astype(BF16),
        'w_q_up': w_q_up_pad.astype(BF16),
    }
    w_uk = w_kv_up[..., :QK_NOPE_DIM]
    w_uv = w_kv_up[..., QK_NOPE_DIM:]
    w_kv_up_cat = jnp.concatenate([w_uk.reshape(KV_LORA_RANK, -1), w_uv.reshape(KV_LORA_RANK, -1)],
                                  axis=1).astype(BF16)
    w_uk_t = jnp.transpose(w_uk, (1, 2, 0)).astype(BF16)
    w_uv_h = jnp.transpose(w_uv, (1, 0, 2)).astype(BF16)
    tables = [_group_bias(t5_bias, g) for g in range(N_GROUPS)]

    mem_kv_prompt = [norm_matmul(mem_prompt[0], g_mem[l], w_mem_kv[l].astype(BF16)) for l in range(depth)]

    def dswa_p(z):
        parts = [dswa_prompt(z, g, _prompt_bias(tables[g])) for g in range(N_GROUPS)]
        return combine_groups([o for o, _ in parts], [l for _, l in parts])

    def mla_p(q, rows, c_bf, kr_bf):
        kv = norm_matmul(c_bf, jnp.ones((KV_LORA_RANK,), F32), w_kv_up_cat, norm=False, out_dtype=BF16)
        return mla_prompt(q, kv, kr_bf)

    y_p, rows_p, z_p = _trunk(x_prompt[0], _rope_tables(jnp.arange(s_len, dtype=jnp.int32)), dswa_p,
                              lambda l, z, cb: mem_attn_prompt(z, cb, mem_kv_prompt[l]), mla_p, p)
    swa_p = []
    for g, (w, _) in enumerate(DSWA_GROUPS):
        k_g = z_p[s_len - w:, (N_GROUPS + g) * GROUP_WIDTH:(N_GROUPS + g + 1) * GROUP_WIDTH]
        v_g = z_p[s_len - w:, (2 * N_GROUPS + g) * GROUP_WIDTH:(2 * N_GROUPS + g + 1) * GROUP_WIDTH]
        kv_g = jnp.stack([k_g.reshape(w, HEADS_PER_GROUP, HEAD_DIM), v_g.reshape(w, HEADS_PER_GROUP, HEAD_DIM)], axis=1)
        swa_p.append(kv_g[None, None])
    mem_kv_out = jnp.stack(mem_kv_prompt).reshape(depth, 1, mem_prompt.shape[1], 2, N_MEM_HEADS, HEAD_DIM)

    ns = bs * t_new
    pos_s = past + (jnp.arange(ns, dtype=jnp.int32) % t_new)
    swa_s = []

    def heads(a, nh):
        return a.reshape(bs, t_new, nh, HEAD_DIM)

    def dswa_s(z):
        outs, lses = [], []
        for g, (w, d) in enumerate(DSWA_GROUPS):
            sec = lambda c: z[:, (c * N_GROUPS + g) * GROUP_WIDTH:(c * N_GROUPS + g + 1) * GROUP_WIDTH]
            new_kv = jnp.concatenate([heads(sec(1), HEADS_PER_GROUP), heads(sec(2), HEADS_PER_GROUP)], axis=2)
            bias_a, bias_b = _sample_bias(tables[g], d, t_new)
            buf = bufs[g][0].reshape(bs, w, 2 * HEADS_PER_GROUP, HEAD_DIM)
            nbuf, o, lse = dswa_sample(buf, new_kv, heads(sec(0), HEADS_PER_GROUP), bias_a, bias_b, d)
            swa_s.append(nbuf.reshape(bufs[g].shape))
            outs.append(o.reshape(ns, GROUP_WIDTH))
            lses.append(lse.reshape(ns, GROUP_WIDTH))
        return combine_groups(outs, lses)

    cache_mem = cache_mem_kv.reshape(depth, bs, cache_mem_kv.shape[2], 2 * N_MEM_HEADS, HEAD_DIM)

    def mem_s(l, z, cb):
        q = heads(z[:, cb * MEM_WIDTH:(cb + 1) * MEM_WIDTH], N_MEM_HEADS)
        return mem_attn_sample(cache_mem, l, q).reshape(ns, MEM_WIDTH)

    def mla_s(q, rows, c_bf, kr_bf):
        q_lat = head_matmul(q, w_uk_t, x_block_stride=Q_HEAD_PAD // QK_NOPE_DIM, out_dtype=BF16)
        q3 = q.reshape(ns, N_B_HEADS, Q_HEAD_PAD)
        q_full = jnp.concatenate([q_lat.reshape(ns, N_B_HEADS, KV_LORA_RANK),
                                  q3[:, :, QK_NOPE_DIM:QK_NOPE_DIM + QK_ROPE_DIM]], axis=-1)
        q_full = q_full.reshape(bs, t_new * N_B_HEADS, MLA_ROW)
        new_rows = rows[:, :MLA_ROW].reshape(bs, t_new, MLA_ROW)
        o_lat = mla_sample(q_full, new_rows, cache_mla_kv, page_table)
        return head_matmul(o_lat.reshape(ns, N_B_HEADS * KV_LORA_RANK), w_uv_h, out_dtype=BF16)

    y_s, rows_s, _ = _trunk(x_sample.reshape(ns, d_model), _rope_tables(pos_s), dswa_s, mem_s, mla_s, p)

    return (y_p[None], y_s.reshape(bs, t_new, d_model), swa_p[0], swa_p[1], swa_p[2],
            rows_p[None], mem_kv_out, swa_s[0], swa_s[1], swa_s[2], rows_s.reshape(bs, t_new, MLA_ROW))
```

```python
import functools
import math

import jax
import jax.numpy as jnp
from jax import lax
from jax.experimental import pallas as pl
from jax.experimental.pallas import tpu as pltpu

F32 = jnp.float32
BF16 = jnp.bfloat16

HEAD_DIM = 128
DSWA_GROUPS = ((128, 1), (512, 4), (2048, 16))
N_GROUPS = len(DSWA_GROUPS)
HEADS_PER_GROUP = 4
GROUP_WIDTH = HEADS_PER_GROUP * HEAD_DIM
N_A_HEADS = N_GROUPS * HEADS_PER_GROUP
A_QKV_WIDTH = 3 * N_A_HEADS * HEAD_DIM
N_MEM_HEADS = 4
MEM_WIDTH = N_MEM_HEADS * HEAD_DIM
N_B_HEADS = 12
Q_LORA_RANK = 1536
KV_LORA_RANK = 512
QK_NOPE_DIM = 128
QK_ROPE_DIM = 64
V_HEAD_DIM = 128
MLA_ROW = KV_LORA_RANK + QK_ROPE_DIM
MLA_SCALE = (QK_NOPE_DIM + QK_ROPE_DIM) ** -0.5
ATTN_SCALE = HEAD_DIM ** -0.5
ROPE_THETA = 10000.0
N_BUCKETS = 32
MAX_DISTANCE = 2048
EPS = 1e-6
PAGE_SIZE = 128

LANES = 128
ROPE_PAD = LANES
Q_HEAD_PAD = QK_NOPE_DIM + ROPE_PAD
MLA_ROW_PAD = KV_LORA_RANK + ROPE_PAD
NEG = -1e30
VMEM_BYTES_V7X = 64 * 1024 * 1024
VMEM_CAP = VMEM_BYTES_V7X - 8 * 1024 * 1024
PAGES_PER_STEP = 8
SEQS_PER_STEP = 2


def _tile(n, pref):
    return pref if n % pref == 0 else n


def _nbytes(shape, dtype):
    return math.prod(shape) * jnp.dtype(dtype).itemsize


def _params(semantics, pipelined_bytes, resident_bytes=0):
    est = 2 * pipelined_bytes + 2 * resident_bytes + (4 << 20)
    return pltpu.CompilerParams(dimension_semantics=semantics,
                                vmem_limit_bytes=int(min(max(est, 16 << 20), VMEM_CAP)))


def _rms(x, g):
    return x * lax.rsqrt(jnp.mean(x * x, axis=-1, keepdims=True) + EPS) * g


def _rope_pad(r, cc, shi, slo):
    half = QK_ROPE_DIM // 2
    return r * cc + pltpu.roll(r, half, axis=1) * shi + pltpu.roll(r, LANES - half, axis=1) * slo


def _dot(a, b):
    return jnp.dot(a, b, preferred_element_type=F32)


def _dot_nt(a, b):
    return lax.dot_general(a, b, (((1,), (1,)), ((), ())), preferred_element_type=F32)


def _norm_matmul_kernel(*refs, norm, rope, scale):
    if rope:
        x_ref, g_ref, w_ref, cc_ref, shi_ref, slo_ref, o_ref, xn_ref = refs
    else:
        x_ref, g_ref, w_ref, o_ref, xn_ref = refs

    @pl.when(pl.program_id(1) == 0)
    def _():
        x = x_ref[...].astype(F32)
        if norm:
            x = _rms(x, g_ref[...])
        xn_ref[...] = x.astype(BF16)

    acc = _dot(xn_ref[...], w_ref[...])
    if scale != 1.0:
        acc = acc * scale
    if rope:
        cc, shi, slo = cc_ref[...], shi_ref[...], slo_ref[...]
        for h in range(acc.shape[1] // Q_HEAD_PAD):
            lo = h * Q_HEAD_PAD
            o_ref[:, lo:lo + QK_NOPE_DIM] = acc[:, lo:lo + QK_NOPE_DIM].astype(o_ref.dtype)
            r = _rope_pad(acc[:, lo + QK_NOPE_DIM:lo + Q_HEAD_PAD], cc, shi, slo)
            o_ref[:, lo + QK_NOPE_DIM:lo + Q_HEAD_PAD] = r.astype(o_ref.dtype)
    else:
        o_ref[...] = acc.astype(o_ref.dtype)


def norm_matmul(x, g, w, *, k_width=None, norm=True, rope_tables=None, scale=1.0,
                out_dtype=F32, tm=512, tn=512):
    n = x.shape[0]
    k = k_width or x.shape[1]
    nout = w.shape[1]
    tm = _tile(n, tm)
    tn = _tile(nout, tn)
    rope = rope_tables is not None
    in_specs = [pl.BlockSpec((tm, k), lambda i, j: (i, 0)),
                pl.BlockSpec((1, k), lambda i, j: (0, 0)),
                pl.BlockSpec((k, tn), lambda i, j: (0, j))]
    args = [x, g.reshape(1, k).astype(F32), w]
    if rope:
        in_specs += [pl.BlockSpec((tm, LANES), lambda i, j: (i, 0))] * 3
        args += list(rope_tables)
    blocks = (_nbytes((tm, k), x.dtype) + _nbytes((k, tn), w.dtype) + _nbytes((tm, tn), out_dtype)
              + 3 * _nbytes((tm, LANES), F32))
    return pl.pallas_call(
        functools.partial(_norm_matmul_kernel, norm=norm, rope=rope, scale=scale),
        grid=(n // tm, nout // tn),
        in_specs=in_specs,
        out_specs=pl.BlockSpec((tm, tn), lambda i, j: (i, j)),
        out_shape=jax.ShapeDtypeStruct((n, nout), out_dtype),
        scratch_shapes=[pltpu.VMEM((tm, k), BF16)],
        compiler_params=_params(("parallel", "arbitrary"), blocks,
                                _nbytes((tm, k), F32) + _nbytes((tm, tn), F32)),
        name="norm_matmul",
    )(*args)


def _kv_finish_kernel(raw_ref, g_ref, cc_ref, shi_ref, slo_ref, rows_ref, c_ref, kr_ref):
    c = _rms(raw_ref[:, :KV_LORA_RANK], g_ref[...])
    kr = _rope_pad(raw_ref[:, KV_LORA_RANK:], cc_ref[...], shi_ref[...], slo_ref[...])
    rows_ref[:, :KV_LORA_RANK] = c
    rows_ref[:, KV_LORA_RANK:] = kr
    c_ref[...] = c.astype(BF16)
    kr_ref[...] = kr.astype(BF16)


def kv_finish(raw, g_latent, rope_tables, *, tm=512):
    n = raw.shape[0]
    tm = _tile(n, tm)
    row = lambda w: pl.BlockSpec((tm, w), lambda i: (i, 0))
    blocks = 2 * _nbytes((tm, MLA_ROW_PAD), F32) + 4 * _nbytes((tm, LANES), F32) + _nbytes((tm, KV_LORA_RANK), BF16)
    return pl.pallas_call(
        _kv_finish_kernel,
        grid=(n // tm,),
        in_specs=[row(MLA_ROW_PAD), pl.BlockSpec((1, KV_LORA_RANK), lambda i: (0, 0)),
                  row(LANES), row(LANES), row(LANES)],
        out_specs=[row(MLA_ROW_PAD), row(KV_LORA_RANK), row(LANES)],
        out_shape=[jax.ShapeDtypeStruct((n, MLA_ROW_PAD), F32),
                   jax.ShapeDtypeStruct((n, KV_LORA_RANK), BF16),
                   jax.ShapeDtypeStruct((n, LANES), BF16)],
        compiler_params=_params(("parallel",), blocks),
        name="kv_finish",
    )(raw, g_latent.reshape(1, -1).astype(F32), *rope_tables)


def _out_proj_kernel(a1_ref, a2_ref, w1_ref, w2_ref, g_ref, x_ref, o_ref):
    o = _dot(a1_ref[...].astype(BF16), w1_ref[...]) + _dot(a2_ref[...].astype(BF16), w2_ref[...])
    o_ref[...] = x_ref[...] + _rms(o, g_ref[...])


def out_proj(a1, a2, w1, w2, g, x, *, tm=512):
    n, d = x.shape
    k1, k2 = a1.shape[1], a2.shape[1]
    tm = _tile(n, tm)
    blocks = (_nbytes((tm, k1), a1.dtype) + _nbytes((tm, k2), a2.dtype) + _nbytes((k1 + k2, d), BF16)
              + 2 * _nbytes((tm, d), F32))
    return pl.pallas_call(
        _out_proj_kernel,
        grid=(n // tm,),
        in_specs=[pl.BlockSpec((tm, k1), lambda i: (i, 0)), pl.BlockSpec((tm, k2), lambda i: (i, 0)),
                  pl.BlockSpec((k1, d), lambda i: (0, 0)), pl.BlockSpec((k2, d), lambda i: (0, 0)),
                  pl.BlockSpec((1, d), lambda i: (0, 0)), pl.BlockSpec((tm, d), lambda i: (i, 0))],
        out_specs=pl.BlockSpec((tm, d), lambda i: (i, 0)),
        out_shape=jax.ShapeDtypeStruct((n, d), F32),
        compiler_params=_params(("parallel",), blocks, _nbytes((tm, d), F32)),
        name="out_proj",
    )(a1, a2, w1, w2, g.reshape(1, d).astype(F32), x)


def _mlp_kernel(x_ref, gpre_ref, wup_ref, wdn_ref, gpost_ref, o_ref, xn_ref, acc_ref):
    j = pl.program_id(1)

    @pl.when(j == 0)
    def _():
        xn_ref[...] = _rms(x_ref[...], gpre_ref[...]).astype(BF16)

    h = _dot(xn_ref[...], wup_ref[...])
    u = jnp.square(jnp.maximum(h, 0.0)).astype(BF16)
    part = _dot(u, wdn_ref[...])

    @pl.when(j == 0)
    def _():
        acc_ref[...] = part

    @pl.when(j > 0)
    def _():
        acc_ref[...] += part

    @pl.when(j == pl.num_programs(1) - 1)
    def _():
        o_ref[...] = x_ref[...] + _rms(acc_ref[...], gpost_ref[...])


def mlp(x, g_pre, w_up, w_down, g_post, *, tm=512, tf=1024):
    n, d = x.shape
    f = w_up.shape[1]
    tm = _tile(n, tm)
    tf = _tile(f, tf)
    blocks = 2 * _nbytes((tm, d), F32) + 2 * _nbytes((d, tf), BF16)
    resident = _nbytes((tm, d), BF16) + _nbytes((tm, d), F32) + _nbytes((tm, tf), F32)
    return pl.pallas_call(
        _mlp_kernel,
        grid=(n // tm, f // tf),
        in_specs=[pl.BlockSpec((tm, d), lambda i, j: (i, 0)), pl.BlockSpec((1, d), lambda i, j: (0, 0)),
                  pl.BlockSpec((d, tf), lambda i, j: (0, j)), pl.BlockSpec((tf, d), lambda i, j: (j, 0)),
                  pl.BlockSpec((1, d), lambda i, j: (0, 0))],
        out_specs=pl.BlockSpec((tm, d), lambda i, j: (i, 0)),
        out_shape=jax.ShapeDtypeStruct((n, d), F32),
        scratch_shapes=[pltpu.VMEM((tm, d), BF16), pltpu.VMEM((tm, d), F32)],
        compiler_params=_params(("parallel", "arbitrary"), blocks, resident),
        name="mlp",
    )(x, g_pre.reshape(1, d).astype(F32), w_up, w_down, g_post.reshape(1, d).astype(F32))


def _head_matmul_kernel(x_ref, w_ref, o_ref):
    o_ref[...] = _dot(x_ref[...].astype(BF16), w_ref[0]).astype(o_ref.dtype)


def head_matmul(x, w, *, x_block_stride=1, out_dtype=F32):
    n = x.shape[0]
    nh, kh, nn = w.shape
    blocks = _nbytes((n, kh), x.dtype) + _nbytes((kh, nn), w.dtype) + _nbytes((n, nn), out_dtype)
    return pl.pallas_call(
        _head_matmul_kernel,
        grid=(nh,),
        in_specs=[pl.BlockSpec((n, kh), lambda h: (0, h * x_block_stride)),
                  pl.BlockSpec((1, kh, nn), lambda h: (h, 0, 0))],
        out_specs=pl.BlockSpec((n, nn), lambda h: (0, h)),
        out_shape=jax.ShapeDtypeStruct((n, nh * nn), out_dtype),
        compiler_params=_params(("parallel",), blocks),
        name="head_matmul",
    )(x, w)


def _combine_kernel(o0, o1, o2, l0, l1, l2, out_ref):
    m = jnp.maximum(jnp.maximum(l0[...], l1[...]), l2[...])
    w0, w1, w2 = jnp.exp(l0[...] - m), jnp.exp(l1[...] - m), jnp.exp(l2[...] - m)
    out_ref[...] = (w0 * o0[...] + w1 * o1[...] + w2 * o2[...]) / (w0 + w1 + w2)


def combine_groups(outs, lses, *, tm=512):
    n, w = outs[0].shape
    tm = _tile(n, tm)
    spec = pl.BlockSpec((tm, w), lambda i: (i, 0))
    return pl.pallas_call(
        _combine_kernel,
        grid=(n // tm,),
        in_specs=[spec] * 6,
        out_specs=spec,
        out_shape=jax.ShapeDtypeStruct((n, w), F32),
        compiler_params=_params(("parallel",), 7 * _nbytes((tm, w), F32)),
        name="combine_groups",
    )(*outs, *lses)


def _dswa_prompt_kernel(q_ref, kp_ref, kc_ref, vp_ref, vc_ref, bias_ref, o_ref, lse_ref):
    blk = q_ref.shape[0]
    col = lax.broadcasted_iota(jnp.int32, (blk, 2 * blk), 1)
    prev_invalid = col < jnp.where(pl.program_id(1) == 0, blk, 0)
    for h in range(HEADS_PER_GROUP):
        sl = slice(h * HEAD_DIM, (h + 1) * HEAD_DIM)
        q = q_ref[:, sl].astype(BF16)
        k = jnp.concatenate([kp_ref[:, sl], kc_ref[:, sl]], axis=0).astype(BF16)
        v = jnp.concatenate([vp_ref[:, sl], vc_ref[:, sl]], axis=0).astype(BF16)
        s = _dot_nt(q, k) * ATTN_SCALE + bias_ref[h]
        s = jnp.where(prev_invalid, NEG, s)
        m = jnp.max(s, axis=-1, keepdims=True)
        e = jnp.exp(s - m)
        l = jnp.sum(e, axis=-1, keepdims=True)
        acc = _dot(e.astype(BF16), v)
        o_ref[:, sl] = acc / l
        lse_ref[:, sl] = jnp.broadcast_to(m + jnp.log(l), (blk, HEAD_DIM))


def dswa_prompt(z, g, bias):
    s_len, zw = z.shape
    w, d = DSWA_GROUPS[g]
    blk = w // d
    sub = s_len // d
    nblk = sub // blk
    zv = z.reshape(sub, d * zw)
    per_res = zw // GROUP_WIDTH
    q_col, k_col, v_col = g, N_GROUPS + g, 2 * N_GROUPS + g
    cur = lambda c: pl.BlockSpec((blk, GROUP_WIDTH), lambda r, u: (u, r * per_res + c))
    prev = lambda c: pl.BlockSpec((blk, GROUP_WIDTH), lambda r, u: (jnp.maximum(u - 1, 0), r * per_res + c))
    out_spec = pl.BlockSpec((blk, GROUP_WIDTH), lambda r, u: (u, r))
    blocks = 7 * _nbytes((blk, GROUP_WIDTH), F32) + _nbytes(bias.shape, F32)
    o, lse = pl.pallas_call(
        _dswa_prompt_kernel,
        grid=(d, nblk),
        in_specs=[cur(q_col), prev(k_col), cur(k_col), prev(v_col), cur(v_col),
                  pl.BlockSpec(bias.shape, lambda r, u: (0, 0, 0))],
        out_specs=[out_spec, out_spec],
        out_shape=[jax.ShapeDtypeStruct((sub, d * GROUP_WIDTH), F32)] * 2,
        compiler_params=_params(("parallel", "arbitrary"), blocks),
        name=f"dswa_prompt_g{g}",
    )(zv, zv, zv, zv, zv, bias)
    return o.reshape(s_len, GROUP_WIDTH), lse.reshape(s_len, GROUP_WIDTH)


def _mem_attn_prompt_kernel(q_ref, kv_ref, o_ref):
    for h in range(N_MEM_HEADS):
        sl = slice(h * HEAD_DIM, (h + 1) * HEAD_DIM)
        q = q_ref[:, sl].astype(BF16)
        k = kv_ref[:, sl].astype(BF16)
        v = kv_ref[:, MEM_WIDTH + h * HEAD_DIM:MEM_WIDTH + (h + 1) * HEAD_DIM].astype(BF16)
        s = _dot_nt(q, k) * ATTN_SCALE
        m = jnp.max(s, axis=-1, keepdims=True)
        e = jnp.exp(s - m)
        l = jnp.sum(e, axis=-1, keepdims=True)
        o_ref[:, sl] = _dot(e.astype(BF16), v) / l


def mem_attn_prompt(z, col_block, mem_kv, *, tm=512):
    n = z.shape[0]
    tm = _tile(n, tm)
    blocks = 2 * _nbytes((tm, MEM_WIDTH), F32) + _nbytes(mem_kv.shape, F32)
    return pl.pallas_call(
        _mem_attn_prompt_kernel,
        grid=(n // tm,),
        in_specs=[pl.BlockSpec((tm, MEM_WIDTH), lambda i: (i, col_block)),
                  pl.BlockSpec(mem_kv.shape, lambda i: (0, 0))],
        out_specs=pl.BlockSpec((tm, MEM_WIDTH), lambda i: (i, 0)),
        out_shape=jax.ShapeDtypeStruct((n, MEM_WIDTH), F32),
        compiler_params=_params(("parallel",), blocks, _nbytes((tm, mem_kv.shape[0]), F32) * 4),
        name="mem_attn_prompt",
    )(z, mem_kv)


def _matmul_nt_kernel(w_ref, x_ref, o_ref):
    o_ref[...] = _dot_nt(w_ref[...], x_ref[...]).astype(o_ref.dtype)


def matmul_nt(w, x, *, tn=1024, out_dtype=BF16):
    m, k = w.shape
    n = x.shape[0]
    tn = _tile(n, tn)
    blocks = _nbytes((m, k), w.dtype) + _nbytes((tn, k), x.dtype) + _nbytes((m, tn), out_dtype)
    return pl.pallas_call(
        _matmul_nt_kernel,
        grid=(n // tn,),
        in_specs=[pl.BlockSpec((m, k), lambda j: (0, 0)), pl.BlockSpec((tn, k), lambda j: (j, 0))],
        out_specs=pl.BlockSpec((m, tn), lambda j: (0, j)),
        out_shape=jax.ShapeDtypeStruct((m, n), out_dtype),
        compiler_params=_params(("parallel",), blocks, _nbytes((m, tn), F32)),
        name="matmul_nt",
    )(w, x)


def _mla_prompt_kernel(q_ref, kn_ref, vt_ref, kr_ref, o_ref, kcat_ref, vt1_ref, *, tq, hps):
    qi = pl.program_id(1)
    nv = V_HEAD_DIM

    @pl.when(qi == 0)
    def _():
        for a in range(hps):
            kcat_ref[a, :, :QK_NOPE_DIM] = kn_ref[:, a * QK_NOPE_DIM:(a + 1) * QK_NOPE_DIM]
            kcat_ref[a, :, QK_NOPE_DIM:] = kr_ref[...]
            vt1_ref[a, :nv, :] = vt_ref[a * nv:(a + 1) * nv, :]
            vt1_ref[a, nv:, :] = jnp.ones((vt1_ref.shape[1] - nv, vt1_ref.shape[2]), BF16)

    qs = [q_ref[:, a * Q_HEAD_PAD:(a + 1) * Q_HEAD_PAD] for a in range(hps)]

    def step(kb, carry, masked):
        start = pl.multiple_of(kb * tq, tq)
        scores = [_dot_nt(kcat_ref[a, pl.ds(start, tq), :], qs[a]) for a in range(hps)]
        out = []
        for a in range(hps):
            m, acc = carry[a]
            s = scores[a]
            if masked:
                key = lax.broadcasted_iota(jnp.int32, (tq, tq), 0)
                qry = lax.broadcasted_iota(jnp.int32, (tq, tq), 1)
                s = jnp.where(key <= qry, s, NEG)
            m_new = jnp.maximum(m, jnp.max(s, axis=0, keepdims=True))
            alpha = jnp.exp(m - m_new)
            p = jnp.exp(s - m_new).astype(BF16)
            acc = alpha * acc + _dot(vt1_ref[a, :, pl.ds(start, tq)], p)
            out.append((m_new, acc))
        return tuple(out)

    init = tuple((jnp.full((1, tq), NEG, F32), jnp.zeros((vt1_ref.shape[1], tq), F32)) for _ in range(hps))
    carry = lax.fori_loop(0, qi, lambda kb, c: step(kb, c, False), init)
    carry = step(qi, carry, True)
    for a, (_, acc) in enumerate(carry):
        o_ref[:, a * nv:(a + 1) * nv] = (acc[:nv] / acc[nv:nv + 1]).T.astype(o_ref.dtype)


def mla_prompt(q, kn, vt, kr, *, tq=512, heads_per_step=2):
    s_len = q.shape[0]
    tq = _tile(s_len, tq)
    hps = heads_per_step
    ones_rows = 16
    blocks = (_nbytes((tq, hps * Q_HEAD_PAD), BF16) + (2 * hps + 1) * _nbytes((s_len, LANES), BF16)
              + _nbytes((tq, hps * V_HEAD_DIM), BF16))
    resident = (hps * _nbytes((s_len, Q_HEAD_PAD), BF16) + hps * _nbytes((V_HEAD_DIM + ones_rows, s_len), BF16)
                + 3 * hps * _nbytes((tq, tq), F32))
    return pl.pallas_call(
        functools.partial(_mla_prompt_kernel, tq=tq, hps=hps),
        grid=(N_B_HEADS // hps, s_len // tq),
        in_specs=[pl.BlockSpec((tq, hps * Q_HEAD_PAD), lambda h, i: (i, h)),
                  pl.BlockSpec((s_len, hps * QK_NOPE_DIM), lambda h, i: (0, h)),
                  pl.BlockSpec((hps * V_HEAD_DIM, s_len), lambda h, i: (h, 0)),
                  pl.BlockSpec((s_len, ROPE_PAD), lambda h, i: (0, 0))],
        out_specs=pl.BlockSpec((tq, hps * V_HEAD_DIM), lambda h, i: (i, h)),
        out_shape=jax.ShapeDtypeStruct((s_len, N_B_HEADS * V_HEAD_DIM), BF16),
        scratch_shapes=[pltpu.VMEM((hps, s_len, Q_HEAD_PAD), BF16),
                        pltpu.VMEM((hps, V_HEAD_DIM + ones_rows, s_len), BF16)],
        compiler_params=_params(("parallel", "arbitrary"), blocks, resident),
        name="mla_prompt",
    )(q, kn, vt, kr)


def _row_scores(q, k, bias):
    return jnp.sum(k * q[None], axis=-1, keepdims=True) * ATTN_SCALE + bias


def _dswa_sample_kernel(buf_ref, new_ref, q_ref, bias_a_ref, bias_b_ref, nbuf_ref, o_ref, lse_ref, *, d):
    nsub = buf_ref.shape[1]
    nh = HEADS_PER_GROUP
    t_new = new_ref.shape[1]
    for r_out in range(d):
        sh, r_in = divmod(r_out + t_new, d)
        if nsub - sh > 0:
            nbuf_ref[0, 0:nsub - sh, r_out] = buf_ref[0, sh:nsub, r_in]
        for a in range(max(nsub - sh, 0), nsub):
            nbuf_ref[0, a, r_out] = new_ref[0, a * d + r_out + t_new - nsub * d]
    k_new = new_ref[0, :, 0:nh, :]
    v_new = new_ref[0, :, nh:2 * nh, :]
    for i in range(t_new):
        q = q_ref[0, i]
        k_old = buf_ref[0, :, i % d, 0:nh, :]
        v_old = buf_ref[0, :, i % d, nh:2 * nh, :]
        s_a = _row_scores(q, k_old, bias_a_ref[i])
        s_b = _row_scores(q, k_new, bias_b_ref[i])
        m = jnp.maximum(jnp.max(s_a, axis=0), jnp.max(s_b, axis=0))
        e_a = jnp.exp(s_a - m[None])
        e_b = jnp.exp(s_b - m[None])
        l = jnp.sum(e_a, axis=0) + jnp.sum(e_b, axis=0)
        acc = jnp.sum(e_a * v_old, axis=0) + jnp.sum(e_b * v_new, axis=0)
        o_ref[0, i] = acc / l
        lse_ref[0, i] = m + jnp.log(l)


def dswa_sample(buf, new_kv, q, bias_a, bias_b, d):
    b, l_buf = buf.shape[:2]
    t_new = new_kv.shape[1]
    nsub = l_buf // d
    rows = 2 * HEADS_PER_GROUP
    bufv = buf.reshape(b, nsub, d, rows, HEAD_DIM)
    buf_spec = pl.BlockSpec((1, nsub, d, rows, HEAD_DIM), lambda i: (i, 0, 0, 0, 0))
    tok = lambda r: pl.BlockSpec((1, t_new, r, HEAD_DIM), lambda i: (i, 0, 0, 0))
    full = lambda a: pl.BlockSpec(a.shape, lambda i: (0,) * a.ndim)
    blocks = 2 * _nbytes((l_buf, rows, HEAD_DIM), F32) + _nbytes(bias_a.shape, F32)
    nbuf, o, lse = pl.pallas_call(
        functools.partial(_dswa_sample_kernel, d=d),
        grid=(b,),
        in_specs=[buf_spec, tok(rows), tok(HEADS_PER_GROUP), full(bias_a), full(bias_b)],
        out_specs=[buf_spec, tok(HEADS_PER_GROUP), tok(HEADS_PER_GROUP)],
        out_shape=[jax.ShapeDtypeStruct(bufv.shape, F32),
                   jax.ShapeDtypeStruct((b, t_new, HEADS_PER_GROUP, HEAD_DIM), F32),
                   jax.ShapeDtypeStruct((b, t_new, HEADS_PER_GROUP, HEAD_DIM), F32)],
        compiler_params=_params(("parallel",), blocks),
        name=f"dswa_sample_d{d}",
    )(bufv, new_kv, q, bias_a, bias_b)
    return nbuf.reshape(buf.shape), o, lse


def _mem_attn_sample_kernel(kv_ref, q_ref, o_ref):
    nh = N_MEM_HEADS
    nq = q_ref.shape[1]
    n_mem, rows = kv_ref.shape[2], kv_ref.shape[3]
    ncol = n_mem * rows
    col_row = lax.broadcasted_iota(jnp.int32, (nq, ncol), 1) % rows
    q_head = lax.broadcasted_iota(jnp.int32, (nq, ncol), 0) // (nq // nh)
    own = col_row == q_head + nh
    data, scores = [], []
    for j in range(q_ref.shape[0]):
        x = kv_ref[0, j]
        swapped = pltpu.roll(x, nh, axis=1).reshape(ncol, HEAD_DIM).astype(BF16)
        data.append(x.reshape(ncol, HEAD_DIM).astype(BF16))
        scores.append(_dot_nt(q_ref[j].astype(BF16), swapped) * ATTN_SCALE)
    for j, s in enumerate(scores):
        s = jnp.where(own, s, NEG)
        e = jnp.exp(s - jnp.max(s, axis=-1, keepdims=True))
        o_ref[j] = _dot(e.astype(BF16), data[j]) / jnp.sum(e, axis=-1, keepdims=True)


def mem_attn_sample(cache, layer, q, *, seqs_per_step=4):
    _, b, n_mem, rows, _ = cache.shape
    nb = seqs_per_step if b % seqs_per_step == 0 else 1
    tok = pl.BlockSpec((nb,) + q.shape[1:], lambda i: (i, 0, 0))
    return pl.pallas_call(
        _mem_attn_sample_kernel,
        grid=(b // nb,),
        in_specs=[pl.BlockSpec((1, nb, n_mem, rows, HEAD_DIM), lambda i: (layer, i, 0, 0, 0)), tok],
        out_specs=tok,
        out_shape=jax.ShapeDtypeStruct(q.shape, F32),
        compiler_params=_params(("parallel",), nb * _nbytes((n_mem, rows, HEAD_DIM), F32),
                                2 * nb * _nbytes((n_mem, rows, HEAD_DIM), F32)),
        name="mem_attn_sample",
    )(cache, q)


def _mla_sample_kernel(pt_ref, q_ref, new_ref, *rest, n_pages, n_seq, t_new):
    del pt_ref
    page_refs = rest[:n_pages * n_seq]
    o_ref, m_ref, l_ref, acc_ref = rest[n_pages * n_seq:]
    c = pl.program_id(1)
    nq = q_ref.shape[1]

    @pl.when(c == 0)
    def _():
        m_ref[...] = jnp.full(m_ref.shape, NEG, F32)
        l_ref[...] = jnp.zeros(l_ref.shape, F32)
        acc_ref[...] = jnp.zeros(acc_ref.shape, F32)

    def update(j, s_parts, pv_fn):
        m_prev = m_ref[j]
        m_cur = jnp.max(functools.reduce(jnp.maximum, s_parts), axis=-1, keepdims=True)
        m_new = jnp.maximum(m_prev, m_cur)
        alpha = jnp.exp(m_prev - m_new)
        p = [jnp.exp(s - m_new) for s in s_parts]
        l_ref[j] = alpha * l_ref[j] + jnp.sum(functools.reduce(jnp.add, p), axis=-1, keepdims=True)
        m_ref[j] = m_new
        acc_ref[j] = alpha * acc_ref[j] + pv_fn([x.astype(BF16) for x in p])

    pages = [[r[0].astype(BF16) for r in page_refs[j * n_pages:(j + 1) * n_pages]] for j in range(n_seq)]
    scores = [[_dot(q_ref[j], kt) for kt in pages[j]] for j in range(n_seq)]
    for j in range(n_seq):
        update(j, scores[j], lambda p, kts=pages[j]: functools.reduce(
            jnp.add, [_dot_nt(pi, kt[:KV_LORA_RANK]) for pi, kt in zip(p, kts)]))

    @pl.when(c == pl.num_programs(1) - 1)
    def _():
        tok = lax.broadcasted_iota(jnp.int32, (nq, t_new), 0) // (nq // t_new)
        key = lax.broadcasted_iota(jnp.int32, (nq, t_new), 1)
        for j in range(n_seq):
            nr = new_ref[j].astype(BF16)
            s = jnp.where(key <= tok, _dot_nt(q_ref[j], nr), NEG)
            update(j, [s], lambda p, nr=nr: _dot(p[0], nr[:, :KV_LORA_RANK]))
            o_ref[j] = acc_ref[j] / l_ref[j]


def mla_sample(q, new_rows, cache_t, page_table):
    b, nq, row = q.shape
    t_new = new_rows.shape[1]
    n_seq_pages = page_table.shape[1]
    pps = PAGES_PER_STEP if n_seq_pages % PAGES_PER_STEP == 0 else 1
    nb = SEQS_PER_STEP if b % SEQS_PER_STEP == 0 else 1

    def page_spec(j, p):
        return pl.BlockSpec((1, row, PAGE_SIZE), lambda i, c, pt: (pt[i * nb + j, c * pps + p], 0, 0))

    blocks = (nb * pps * _nbytes((row, PAGE_SIZE), F32) + nb * _nbytes((nq, row), BF16)
              + nb * _nbytes((nq, KV_LORA_RANK), F32))
    return pl.pallas_call(
        functools.partial(_mla_sample_kernel, n_pages=pps, n_seq=nb, t_new=t_new),
        grid_spec=pltpu.PrefetchScalarGridSpec(
            num_scalar_prefetch=1,
            grid=(b // nb, n_seq_pages // pps),
            in_specs=[pl.BlockSpec((nb, nq, row), lambda i, c, pt: (i, 0, 0)),
                      pl.BlockSpec((nb, t_new, row), lambda i, c, pt: (i, 0, 0))]
                     + [page_spec(j, p) for j in range(nb) for p in range(pps)],
            out_specs=pl.BlockSpec((nb, nq, KV_LORA_RANK), lambda i, c, pt: (i, 0, 0)),
            scratch_shapes=[pltpu.VMEM((nb, nq, 1), F32), pltpu.VMEM((nb, nq, 1), F32),
                            pltpu.VMEM((nb, nq, KV_LORA_RANK), F32)]),
        out_shape=jax.ShapeDtypeStruct((b, nq, KV_LORA_RANK), F32),
        compiler_params=_params(("parallel", "arbitrary"), blocks,
                                nb * pps * _nbytes((row, PAGE_SIZE), BF16)),
        name="mla_sample",
    )(page_table, q, new_rows, *([cache_t] * (nb * pps)))


def _t5_bucket(dist):
    max_exact = N_BUCKETS // 2
    dd = jnp.maximum(dist, 1).astype(F32)
    large = max_exact + (jnp.log(dd / max_exact) / math.log(MAX_DISTANCE / max_exact)
                         * (N_BUCKETS - max_exact)).astype(jnp.int32)
    large = jnp.minimum(large, N_BUCKETS - 1)
    return jnp.where(dist < max_exact, dist, large)


def _group_bias(t5_bias, g):
    w, d = DSWA_GROUPS[g]
    offs = jnp.arange(w // d + 1, dtype=jnp.int32) * d
    return t5_bias[_t5_bucket(offs), g * HEADS_PER_GROUP:(g + 1) * HEADS_PER_GROUP].astype(F32)


def _banded(table, j):
    jmax = table.shape[0] - 1
    vals = table[jnp.clip(j, 0, jmax)]
    return jnp.where(((j >= 0) & (j <= jmax))[..., None], vals, NEG)


def _prompt_bias(table):
    blk, nh = table.shape[0] - 1, table.shape[1]
    n = 3 * blk - 1
    pad = jnp.full((blk - 1, nh), NEG, F32)
    u = jnp.concatenate([pad, table, pad], axis=0)
    skew = jnp.tile(u, (blk + 1, 1))[:blk * (n + 1)].reshape(blk, n + 1, nh)
    return jnp.moveaxis(skew[:, :2 * blk][:, ::-1], -1, 0)


def _sample_bias(table, d, t_new):
    blk = table.shape[0] - 1
    i = jnp.arange(t_new)
    j_a = blk - jnp.arange(blk)[None, :] + (i // d)[:, None]
    diff = i[:, None] - i[None, :]
    j_b = jnp.where((diff >= 0) & (diff % d == 0), diff // d, -1)
    lanes = lambda x: jnp.broadcast_to(x[..., None], x.shape + (HEAD_DIM,))
    return lanes(_banded(table, j_a)), lanes(_banded(table, j_b))


def _rope_tables(pos):
    half = QK_ROPE_DIM // 2
    inv_freq = ROPE_THETA ** (-jnp.arange(half, dtype=F32) / half)
    ang = pos.astype(F32)[:, None] * inv_freq[None, :]
    cos, sin = jnp.cos(ang), jnp.sin(ang)
    zero = jnp.zeros_like(cos)
    cc = jnp.concatenate([cos, cos, zero, zero], axis=-1)
    shi = jnp.concatenate([zero, sin, zero, zero], axis=-1)
    slo = jnp.concatenate([-sin, zero, zero, zero], axis=-1)
    return cc, shi, slo


def _pad_last(w, to):
    return jnp.pad(w, [(0, 0)] * (w.ndim - 1) + [(0, to - w.shape[-1])])


def _trunk(x, rope_tables, dswa_fn, mem_fn, mla_fn, p):
    n = x.shape[0]
    z0 = norm_matmul(x, p['g_attn_pre'][0], p['w_a_in'])
    o_tok = dswa_fn(z0)
    o_mem = mem_fn(0, z0, A_QKV_WIDTH // MEM_WIDTH)
    x = out_proj(o_tok, o_mem, p['w_a_out'][:GROUP_WIDTH], p['w_a_out'][GROUP_WIDTH:], p['g_attn_post'][0], x)
    x = mlp(x, p['g_mlp_pre'][0], p['w_mlp_up'][0], p['w_mlp_down'][0], p['g_mlp_post'][0])
    raw = norm_matmul(x, p['g_kv_in'], p['w_kv_down'])
    rows, c_bf, kr_bf = kv_finish(raw, p['g_kv_latent'], rope_tables)
    z = norm_matmul(x, p['g_attn_pre'][1], p['w_b_in'])
    q = norm_matmul(z, p['g_q_latent'], p['w_q_up'], k_width=Q_LORA_RANK, rope_tables=rope_tables,
                    scale=MLA_SCALE, out_dtype=BF16, tn=2 * Q_HEAD_PAD)
    o_tok = mla_fn(q, rows, c_bf, kr_bf)
    o_mem = mem_fn(1, z, Q_LORA_RANK // MEM_WIDTH)
    nv = N_B_HEADS * V_HEAD_DIM
    x = out_proj(o_tok, o_mem, p['w_b_out'][:nv], p['w_b_out'][nv:], p['g_attn_post'][1], x)
    x = mlp(x, p['g_mlp_pre'][1], p['w_mlp_up'][1], p['w_mlp_down'][1], p['g_mlp_post'][1])
    return x, rows[:, :MLA_ROW], z0


def kernel(x_prompt, x_sample, mem_prompt, cache_swa_kv_w128, cache_swa_kv_w512, cache_swa_kv_w2048, cache_mla_kv, cache_mem_kv, page_table, t5_bias, g_attn_pre, g_attn_post, g_mlp_pre, g_mlp_post, g_mem, w_mem_kv, w_mlp_up, w_mlp_down, w_a_in, w_a_out, g_kv_in, w_kv_down, g_kv_latent, w_kv_up, w_b_in, g_q_latent, w_q_up, w_b_out):
    depth = g_attn_pre.shape[0]
    assert depth == 2 and w_a_in.shape[0] == 1 and w_b_in.shape[0] == 1
    bp, s_len, d_model = x_prompt.shape
    assert bp == 1
    bs, t_new, _ = x_sample.shape
    past = page_table.shape[1] * PAGE_SIZE
    bufs = (cache_swa_kv_w128, cache_swa_kv_w512, cache_swa_kv_w2048)
    for buf, (w, d) in zip(bufs, DSWA_GROUPS):
        assert buf.shape[2] == w and past >= w and s_len % (d * (w // d)) == 0 and s_len >= w

    w_q_up_pad = _pad_last(w_q_up[0].reshape(Q_LORA_RANK, N_B_HEADS, QK_NOPE_DIM + QK_ROPE_DIM),
                           Q_HEAD_PAD).reshape(Q_LORA_RANK, N_B_HEADS * Q_HEAD_PAD)
    p = {
        'g_attn_pre': g_attn_pre, 'g_attn_post': g_attn_post, 'g_mlp_pre': g_mlp_pre, 'g_mlp_post': g_mlp_post,
        'g_kv_in': g_kv_in, 'g_kv_latent': g_kv_latent, 'g_q_latent': g_q_latent[0],
        'w_a_in': w_a_in[0].astype(BF16), 'w_a_out': w_a_out[0].astype(BF16),
        'w_b_in': w_b_in[0].astype(BF16), 'w_b_out': w_b_out[0].astype(BF16),
        'w_mlp_up': w_mlp_up.astype(BF16), 'w_mlp_down': w_mlp_down.astype(BF16),
        'w_kv_down': _pad_last(w_kv_down, MLA_ROW_PAD).astype(BF16),
        'w_q_up': w_q_up_pad.astype(BF16),
    }
    w_uk = w_kv_up[..., :QK_NOPE_DIM]
    w_uv = w_kv_up[..., QK_NOPE_DIM:]
    w_uk_cat = w_uk.reshape(KV_LORA_RANK, -1).astype(BF16)
    w_uv_cat_t = w_uv.reshape(KV_LORA_RANK, -1).T.astype(BF16)
    w_uk_t =jnp.transpose(w_uk, (1, 2, 0)).astype(BF16)
    w_uv_h = jnp.transpose(w_uv, (1, 0, 2)).astype(BF16)
    tables = [_group_bias(t5_bias, g) for g in range(N_GROUPS)]

    mem_kv_prompt = [norm_matmul(mem_prompt[0], g_mem[l], w_mem_kv[l].astype(BF16)) for l in range(depth)]

    def dswa_p(z):
        parts = [dswa_prompt(z, g, _prompt_bias(tables[g])) for g in range(N_GROUPS)]
        return combine_groups([o for o, _ in parts], [l for _, l in parts])

    def mla_p(q, rows, c_bf, kr_bf):
        kn = norm_matmul(c_bf, jnp.ones((KV_LORA_RANK,), F32), w_uk_cat, norm=False, out_dtype=BF16)
        vt = matmul_nt(w_uv_cat_t, c_bf)
        return mla_prompt(q, kn, vt, kr_bf)

    y_p, rows_p, z_p = _trunk(x_prompt[0], _rope_tables(jnp.arange(s_len, dtype=jnp.int32)), dswa_p,
                              lambda l, z, cb: mem_attn_prompt(z, cb, mem_kv_prompt[l]), mla_p, p)
    swa_p = []
    for g, (w, _) in enumerate(DSWA_GROUPS):
        k_g = z_p[s_len - w:, (N_GROUPS + g) * GROUP_WIDTH:(N_GROUPS + g + 1) * GROUP_WIDTH]
        v_g = z_p[s_len - w:, (2 * N_GROUPS + g) * GROUP_WIDTH:(2 * N_GROUPS + g + 1) * GROUP_WIDTH]
        kv_g = jnp.stack([k_g.reshape(w, HEADS_PER_GROUP, HEAD_DIM), v_g.reshape(w, HEADS_PER_GROUP, HEAD_DIM)], axis=1)
        swa_p.append(kv_g[None, None])
    mem_kv_out = jnp.stack(mem_kv_prompt).reshape(depth, 1, mem_prompt.shape[1], 2, N_MEM_HEADS, HEAD_DIM)

    ns = bs * t_new
    pos_s = past + (jnp.arange(ns, dtype=jnp.int32) % t_new)
    swa_s = []

    def heads(a, nh):
        return a.reshape(bs, t_new, nh, HEAD_DIM)

    def dswa_s(z):
        outs, lses = [], []
        for g, (w, d) in enumerate(DSWA_GROUPS):
            sec = lambda c: z[:, (c * N_GROUPS + g) * GROUP_WIDTH:(c * N_GROUPS + g + 1) * GROUP_WIDTH]
            new_kv = jnp.concatenate([heads(sec(1), HEADS_PER_GROUP), heads(sec(2), HEADS_PER_GROUP)], axis=2)
            bias_a, bias_b = _sample_bias(tables[g], d, t_new)
            buf = bufs[g][0].reshape(bs, w, 2 * HEADS_PER_GROUP, HEAD_DIM)
            nbuf, o, lse = dswa_sample(buf, new_kv, heads(sec(0), HEADS_PER_GROUP), bias_a, bias_b, d)
            swa_s.append(nbuf.reshape(bufs[g].shape))
            outs.append(o.reshape(ns, GROUP_WIDTH))
            lses.append(lse.reshape(ns, GROUP_WIDTH))
        return combine_groups(outs, lses)

    cache_mem = cache_mem_kv.reshape(depth, bs, cache_mem_kv.shape[2], 2 * N_MEM_HEADS, HEAD_DIM)

    def mem_s(l, z, cb):
        q = heads(z[:, cb * MEM_WIDTH:(cb + 1) * MEM_WIDTH], N_MEM_HEADS)
        q = jnp.swapaxes(q, 1, 2).reshape(bs, N_MEM_HEADS * t_new, HEAD_DIM)
        o = mem_attn_sample(cache_mem, l, q).reshape(bs, N_MEM_HEADS, t_new, HEAD_DIM)
        return jnp.swapaxes(o, 1, 2).reshape(ns, MEM_WIDTH)

    cache_mla_t = jnp.swapaxes(cache_mla_kv, 1, 2)

    def mla_s(q, rows, c_bf, kr_bf):
        q_lat = head_matmul(q, w_uk_t, x_block_stride=Q_HEAD_PAD // QK_NOPE_DIM, out_dtype=BF16)
        q3 = q.reshape(ns, N_B_HEADS, Q_HEAD_PAD)
        q_full = jnp.concatenate([q_lat.reshape(ns, N_B_HEADS, KV_LORA_RANK),
                                  q3[:, :, QK_NOPE_DIM:QK_NOPE_DIM + QK_ROPE_DIM]], axis=-1)
        q_full = q_full.reshape(bs, t_new * N_B_HEADS, MLA_ROW)
        new_rows = rows[:, :MLA_ROW].reshape(bs, t_new, MLA_ROW)
        o_lat = mla_sample(q_full, new_rows, cache_mla_t, page_table)
        return head_matmul(o_lat.reshape(ns, N_B_HEADS * KV_LORA_RANK), w_uv_h, out_dtype=BF16)

    y_s, rows_s, _ = _trunk(x_sample.reshape(ns, d_model), _rope_tables(pos_s), dswa_s, mem_s, mla_s, p)

    return (y_p[None], y_s.reshape(bs, t_new, d_model), swa_p[0], swa_p[1], swa_p[2],
            rows_p[None], mem_kv_out, swa_s[0], swa_s[1], swa_s[2], rows_s.reshape(bs, t_new, MLA_ROW))
```

```python
import functools
import math

import jax
import jax.numpy as jnp
from jax import lax
from jax.experimental import pallas as pl
from jax.experimental.pallas import tpu as pltpu

F32 = jnp.float32
BF16 = jnp.bfloat16

HEAD_DIM = 128
DSWA_GROUPS = ((128, 1), (512, 4), (2048, 16))
N_GROUPS = len(DSWA_GROUPS)
HEADS_PER_GROUP = 4
GROUP_WIDTH = HEADS_PER_GROUP * HEAD_DIM
N_A_HEADS = N_GROUPS * HEADS_PER_GROUP
A_QKV_WIDTH = 3 * N_A_HEADS * HEAD_DIM
N_MEM_HEADS = 4
MEM_WIDTH = N_MEM_HEADS * HEAD_DIM
N_B_HEADS = 12
Q_LORA_RANK = 1536
KV_LORA_RANK = 512
QK_NOPE_DIM = 128
QK_ROPE_DIM = 64
V_HEAD_DIM = 128
MLA_ROW = KV_LORA_RANK + QK_ROPE_DIM
MLA_SCALE = (QK_NOPE_DIM + QK_ROPE_DIM) ** -0.5
ATTN_SCALE = HEAD_DIM ** -0.5
ROPE_THETA = 10000.0
N_BUCKETS = 32
MAX_DISTANCE = 2048
EPS = 1e-6
PAGE_SIZE = 128

LANES = 128
ROPE_PAD = LANES
Q_HEAD_PAD = QK_NOPE_DIM + ROPE_PAD
MLA_ROW_PAD = KV_LORA_RANK + ROPE_PAD
NEG = -1e30
VMEM_BYTES_V7X = 64 * 1024 * 1024
VMEM_CAP = VMEM_BYTES_V7X - 8 * 1024 * 1024
PAGES_PER_STEP = 8
SEQS_PER_STEP = 2
DENSE_SAMPLE_MAX_ROWS = 512


def _tile(n, pref):
    return pref if n % pref == 0 else n


def _nbytes(shape, dtype):
    return math.prod(shape) * jnp.dtype(dtype).itemsize


def _params(semantics, pipelined_bytes, resident_bytes=0):
    est = 2 * pipelined_bytes + 2 * resident_bytes + (4 << 20)
    return pltpu.CompilerParams(dimension_semantics=semantics,
                                vmem_limit_bytes=int(min(max(est, 16 << 20), VMEM_CAP)))


def _rms(x, g):
    return x * lax.rsqrt(jnp.mean(x * x, axis=-1, keepdims=True) + EPS) * g


def _rope_pad(r, cc, shi, slo):
    half = QK_ROPE_DIM // 2
    return r * cc + pltpu.roll(r, half, axis=1) * shi + pltpu.roll(r, LANES - half, axis=1) * slo


def _dot(a, b):
    return jnp.dot(a, b, preferred_element_type=F32)


def _dot_nt(a, b):
    return lax.dot_general(a, b, (((1,), (1,)), ((), ())), preferred_element_type=F32)


def _norm_matmul_kernel(*refs, norm, rope, scale):
    if rope:
        x_ref, g_ref, w_ref, cc_ref, shi_ref, slo_ref, o_ref, xn_ref = refs
    else:
        x_ref, g_ref, w_ref, o_ref, xn_ref = refs

    @pl.when(pl.program_id(1) == 0)
    def _():
        x = x_ref[...].astype(F32)
        if norm:
            x = _rms(x, g_ref[...])
        xn_ref[...] = x.astype(BF16)

    acc = _dot(xn_ref[...], w_ref[...])
    if scale != 1.0:
        acc = acc * scale
    if rope:
        cc, shi, slo = cc_ref[...], shi_ref[...], slo_ref[...]
        for h in range(acc.shape[1] // Q_HEAD_PAD):
            lo = h * Q_HEAD_PAD
            o_ref[:, lo:lo + QK_NOPE_DIM] = acc[:, lo:lo + QK_NOPE_DIM].astype(o_ref.dtype)
            r = _rope_pad(acc[:, lo + QK_NOPE_DIM:lo + Q_HEAD_PAD], cc, shi, slo)
            o_ref[:, lo + QK_NOPE_DIM:lo + Q_HEAD_PAD] = r.astype(o_ref.dtype)
    else:
        o_ref[...] = acc.astype(o_ref.dtype)


def norm_matmul(x, g, w, *, k_width=None, norm=True, rope_tables=None, scale=1.0,
                out_dtype=F32, tm=512, tn=512):
    n = x.shape[0]
    k = k_width or x.shape[1]
    nout = w.shape[1]
    tm = _tile(n, tm)
    tn = _tile(nout, tn)
    rope = rope_tables is not None
    in_specs = [pl.BlockSpec((tm, k), lambda i, j: (i, 0)),
                pl.BlockSpec((1, k), lambda i, j: (0, 0)),
                pl.BlockSpec((k, tn), lambda i, j: (0, j))]
    args = [x, g.reshape(1, k).astype(F32), w]
    if rope:
        in_specs += [pl.BlockSpec((tm, LANES), lambda i, j: (i, 0))] * 3
        args += list(rope_tables)
    blocks = (_nbytes((tm, k), x.dtype) + _nbytes((k, tn), w.dtype) + _nbytes((tm, tn), out_dtype)
              + 3 * _nbytes((tm, LANES), F32))
    return pl.pallas_call(
        functools.partial(_norm_matmul_kernel, norm=norm, rope=rope, scale=scale),
        grid=(n // tm, nout // tn),
        in_specs=in_specs,
        out_specs=pl.BlockSpec((tm, tn), lambda i, j: (i, j)),
        out_shape=jax.ShapeDtypeStruct((n, nout), out_dtype),
        scratch_shapes=[pltpu.VMEM((tm, k), BF16)],
        compiler_params=_params(("parallel", "arbitrary"), blocks,
                                _nbytes((tm, k), F32) + _nbytes((tm, tn), F32)),
        name="norm_matmul",
    )(*args)


def _kv_finish_kernel(raw_ref, g_ref, cc_ref, shi_ref, slo_ref, rows_ref, c_ref, kr_ref):
    c = _rms(raw_ref[:, :KV_LORA_RANK], g_ref[...])
    kr = _rope_pad(raw_ref[:, KV_LORA_RANK:], cc_ref[...], shi_ref[...], slo_ref[...])
    rows_ref[:, :KV_LORA_RANK] = c
    rows_ref[:, KV_LORA_RANK:] = kr
    c_ref[...] = c.astype(BF16)
    kr_ref[...] = kr.astype(BF16)


def kv_finish(raw, g_latent, rope_tables, *, tm=512):
    n = raw.shape[0]
    tm = _tile(n, tm)
    row = lambda w: pl.BlockSpec((tm, w), lambda i: (i, 0))
    blocks = 2 * _nbytes((tm, MLA_ROW_PAD), F32) + 4 * _nbytes((tm, LANES), F32) + _nbytes((tm, KV_LORA_RANK), BF16)
    return pl.pallas_call(
        _kv_finish_kernel,
        grid=(n // tm,),
        in_specs=[row(MLA_ROW_PAD), pl.BlockSpec((1, KV_LORA_RANK), lambda i: (0, 0)),
                  row(LANES), row(LANES), row(LANES)],
        out_specs=[row(MLA_ROW_PAD), row(KV_LORA_RANK), row(LANES)],
        out_shape=[jax.ShapeDtypeStruct((n, MLA_ROW_PAD), F32),
                   jax.ShapeDtypeStruct((n, KV_LORA_RANK), BF16),
                   jax.ShapeDtypeStruct((n, LANES), BF16)],
        compiler_params=_params(("parallel",), blocks),
        name="kv_finish",
    )(raw, g_latent.reshape(1, -1).astype(F32), *rope_tables)


def _out_proj_kernel(a1_ref, a2_ref, w1_ref, w2_ref, g_ref, x_ref, o_ref):
    o = _dot(a1_ref[...].astype(BF16), w1_ref[...]) + _dot(a2_ref[...].astype(BF16), w2_ref[...])
    o_ref[...] = x_ref[...] + _rms(o, g_ref[...])


def out_proj(a1, a2, w1, w2, g, x, *, tm=512):
    n, d = x.shape
    k1, k2 = a1.shape[1], a2.shape[1]
    tm = _tile(n, tm)
    blocks = (_nbytes((tm, k1), a1.dtype) + _nbytes((tm, k2), a2.dtype) + _nbytes((k1 + k2, d), BF16)
              + 2 * _nbytes((tm, d), F32))
    return pl.pallas_call(
        _out_proj_kernel,
        grid=(n // tm,),
        in_specs=[pl.BlockSpec((tm, k1), lambda i: (i, 0)), pl.BlockSpec((tm, k2), lambda i: (i, 0)),
                  pl.BlockSpec((k1, d), lambda i: (0, 0)), pl.BlockSpec((k2, d), lambda i: (0, 0)),
                  pl.BlockSpec((1, d), lambda i: (0, 0)), pl.BlockSpec((tm, d), lambda i: (i, 0))],
        out_specs=pl.BlockSpec((tm, d), lambda i: (i, 0)),
        out_shape=jax.ShapeDtypeStruct((n, d), F32),
        compiler_params=_params(("parallel",), blocks, _nbytes((tm, d), F32)),
        name="out_proj",
    )(a1, a2, w1, w2, g.reshape(1, d).astype(F32), x)


def _mlp_kernel(x_ref, gpre_ref, wup_ref, wdn_ref, gpost_ref, o_ref, xn_ref, acc_ref):
    j = pl.program_id(1)

    @pl.when(j == 0)
    def _():
        xn_ref[...] = _rms(x_ref[...], gpre_ref[...]).astype(BF16)

    h = _dot(xn_ref[...], wup_ref[...])
    u = jnp.square(jnp.maximum(h, 0.0)).astype(BF16)
    part = _dot(u, wdn_ref[...])

    @pl.when(j == 0)
    def _():
        acc_ref[...] = part

    @pl.when(j > 0)
    def _():
        acc_ref[...] += part

    @pl.when(j == pl.num_programs(1) - 1)
    def _():
        o_ref[...] = x_ref[...] + _rms(acc_ref[...], gpost_ref[...])


def mlp(x, g_pre, w_up, w_down, g_post, *, tm=512, tf=1024):
    n, d = x.shape
    f = w_up.shape[1]
    tm = _tile(n, tm)
    tf = _tile(f, tf)
    blocks = 2 * _nbytes((tm, d), F32) + 2 * _nbytes((d, tf), BF16)
    resident = _nbytes((tm, d), BF16) + _nbytes((tm, d), F32) + _nbytes((tm, tf), F32)
    return pl.pallas_call(
        _mlp_kernel,
        grid=(n // tm, f // tf),
        in_specs=[pl.BlockSpec((tm, d), lambda i, j: (i, 0)), pl.BlockSpec((1, d), lambda i, j: (0, 0)),
                  pl.BlockSpec((d, tf), lambda i, j: (0, j)), pl.BlockSpec((tf, d), lambda i, j: (j, 0)),
                  pl.BlockSpec((1, d), lambda i, j: (0, 0))],
        out_specs=pl.BlockSpec((tm, d), lambda i, j: (i, 0)),
        out_shape=jax.ShapeDtypeStruct((n, d), F32),
        scratch_shapes=[pltpu.VMEM((tm, d), BF16), pltpu.VMEM((tm, d), F32)],
        compiler_params=_params(("parallel", "arbitrary"), blocks, resident),
        name="mlp",
    )(x, g_pre.reshape(1, d).astype(F32), w_up, w_down, g_post.reshape(1, d).astype(F32))


def _head_matmul_kernel(x_ref, w_ref, o_ref):
    o_ref[...] = _dot(x_ref[...].astype(BF16), w_ref[0]).astype(o_ref.dtype)


def head_matmul(x, w, *, x_block_stride=1, out_dtype=F32):
    n = x.shape[0]
    nh, kh, nn = w.shape
    blocks = _nbytes((n, kh), x.dtype) + _nbytes((kh, nn), w.dtype) + _nbytes((n, nn), out_dtype)
    return pl.pallas_call(
        _head_matmul_kernel,
        grid=(nh,),
        in_specs=[pl.BlockSpec((n, kh), lambda h: (0, h * x_block_stride)),
                  pl.BlockSpec((1, kh, nn), lambda h: (h, 0, 0))],
        out_specs=pl.BlockSpec((n, nn), lambda h: (0, h)),
        out_shape=jax.ShapeDtypeStruct((n, nh * nn), out_dtype),
        compiler_params=_params(("parallel",), blocks),
        name="head_matmul",
    )(x, w)


def _combine_kernel(o0, o1, o2, l0, l1, l2, out_ref):
    m = jnp.maximum(jnp.maximum(l0[...], l1[...]), l2[...])
    w0, w1, w2 = jnp.exp(l0[...] - m), jnp.exp(l1[...] - m), jnp.exp(l2[...] - m)
    out_ref[...] = (w0 * o0[...] + w1 * o1[...] + w2 * o2[...]) / (w0 + w1 + w2)


def combine_groups(outs, lses, *, tm=512):
    n, w = outs[0].shape
    tm = _tile(n, tm)
    spec = pl.BlockSpec((tm, w), lambda i: (i, 0))
    return pl.pallas_call(
        _combine_kernel,
        grid=(n // tm,),
        in_specs=[spec] * 6,
        out_specs=spec,
        out_shape=jax.ShapeDtypeStruct((n, w), F32),
        compiler_params=_params(("parallel",), 7 * _nbytes((tm, w), F32)),
        name="combine_groups",
    )(*outs, *lses)


def _dswa_prompt_kernel(q_ref, kp_ref, kc_ref, vp_ref, vc_ref, bias_ref, o_ref, lse_ref):
    blk = q_ref.shape[0]
    col = lax.broadcasted_iota(jnp.int32, (blk, 2 * blk), 1)
    prev_invalid = col < jnp.where(pl.program_id(1) == 0, blk, 0)
    for h in range(HEADS_PER_GROUP):
        sl = slice(h * HEAD_DIM, (h + 1) * HEAD_DIM)
        q = q_ref[:, sl].astype(BF16)
        k = jnp.concatenate([kp_ref[:, sl], kc_ref[:, sl]], axis=0).astype(BF16)
        v = jnp.concatenate([vp_ref[:, sl], vc_ref[:, sl]], axis=0).astype(BF16)
        s = _dot_nt(q, k) * ATTN_SCALE + bias_ref[h]
        s = jnp.where(prev_invalid, NEG, s)
        m = jnp.max(s, axis=-1, keepdims=True)
        e = jnp.exp(s - m)
        l = jnp.sum(e, axis=-1, keepdims=True)
        acc = _dot(e.astype(BF16), v)
        o_ref[:, sl] = acc / l
        lse_ref[:, sl] = jnp.broadcast_to(m + jnp.log(l), (blk, HEAD_DIM))


def dswa_prompt(z, g, bias):
    s_len, zw = z.shape
    w, d = DSWA_GROUPS[g]
    blk = w // d
    sub = s_len // d
    nblk = sub // blk
    q_col, k_col, v_col = g, N_GROUPS + g, 2 * N_GROUPS + g
    if d == 1:
        zv, per_res = z, zw // GROUP_WIDTH
    else:
        zg = jnp.concatenate([z[:, c * GROUP_WIDTH:(c + 1) * GROUP_WIDTH] for c in (q_col, k_col, v_col)], axis=1)
        zv, per_res = zg.reshape(sub, d * 3 * GROUP_WIDTH), 3
        q_col, k_col, v_col = 0, 1, 2
    cur = lambda c: pl.BlockSpec((blk, GROUP_WIDTH), lambda r, u: (u, r * per_res + c))
    prev = lambda c: pl.BlockSpec((blk, GROUP_WIDTH), lambda r, u: (jnp.maximum(u - 1, 0), r * per_res + c))
    out_spec = pl.BlockSpec((blk, GROUP_WIDTH), lambda r, u: (u, r))
    blocks = 7 * _nbytes((blk, GROUP_WIDTH), F32) + _nbytes(bias.shape, F32)
    o, lse = pl.pallas_call(
        _dswa_prompt_kernel,
        grid=(d, nblk),
        in_specs=[cur(q_col), prev(k_col), cur(k_col), prev(v_col), cur(v_col),
                  pl.BlockSpec(bias.shape, lambda r, u: (0, 0, 0))],
        out_specs=[out_spec, out_spec],
        out_shape=[jax.ShapeDtypeStruct((sub, d * GROUP_WIDTH), F32)] * 2,
        compiler_params=_params(("parallel", "arbitrary"), blocks),
        name=f"dswa_prompt_g{g}",
    )(zv, zv, zv, zv, zv, bias)
    return o.reshape(s_len, GROUP_WIDTH), lse.reshape(s_len, GROUP_WIDTH)


def _mem_attn_prompt_kernel(q_ref, kv_ref, o_ref):
    for h in range(N_MEM_HEADS):
        sl = slice(h * HEAD_DIM, (h + 1) * HEAD_DIM)
        q = q_ref[:, sl].astype(BF16)
        k = kv_ref[:, sl].astype(BF16)
        v = kv_ref[:, MEM_WIDTH + h * HEAD_DIM:MEM_WIDTH + (h + 1) * HEAD_DIM].astype(BF16)
        s = _dot_nt(q, k) * ATTN_SCALE
        m = jnp.max(s, axis=-1, keepdims=True)
        e = jnp.exp(s - m)
        l = jnp.sum(e, axis=-1, keepdims=True)
        o_ref[:, sl] = _dot(e.astype(BF16), v) / l


def mem_attn_prompt(z, col_block, mem_kv, *, tm=512):
    n = z.shape[0]
    tm = _tile(n, tm)
    blocks = 2 * _nbytes((tm, MEM_WIDTH), F32) + _nbytes(mem_kv.shape, F32)
    return pl.pallas_call(
        _mem_attn_prompt_kernel,
        grid=(n // tm,),
        in_specs=[pl.BlockSpec((tm, MEM_WIDTH), lambda i: (i, col_block)),
                  pl.BlockSpec(mem_kv.shape, lambda i: (0, 0))],
        out_specs=pl.BlockSpec((tm, MEM_WIDTH), lambda i: (i, 0)),
        out_shape=jax.ShapeDtypeStruct((n, MEM_WIDTH), F32),
        compiler_params=_params(("parallel",), blocks, _nbytes((tm, mem_kv.shape[0]), F32) * 4),
        name="mem_attn_prompt",
    )(z, mem_kv)


ONES_ROWS = 16


def _head_keys_kernel(c_ref, w_ref, kr_ref, o_ref):
    o_ref[0, :, :QK_NOPE_DIM] = _dot(c_ref[...], w_ref[...]).astype(o_ref.dtype)
    o_ref[0, :, QK_NOPE_DIM:] = kr_ref[...]


def head_keys(c, w_uk, kr, *, tm=1024):
    n, k = c.shape
    tm = _tile(n, tm)
    blocks = _nbytes((tm, k), BF16) + _nbytes((k, QK_NOPE_DIM), BF16) + 3 * _nbytes((tm, LANES), BF16)
    return pl.pallas_call(
        _head_keys_kernel,
        grid=(N_B_HEADS, n // tm),
        in_specs=[pl.BlockSpec((tm, k), lambda h, i: (i, 0)),
                  pl.BlockSpec((k, QK_NOPE_DIM), lambda h, i: (0, h)),
                  pl.BlockSpec((tm, ROPE_PAD), lambda h, i: (i, 0))],
        out_specs=pl.BlockSpec((1, tm, Q_HEAD_PAD), lambda h, i: (h, i, 0)),
        out_shape=jax.ShapeDtypeStruct((N_B_HEADS, n, Q_HEAD_PAD), BF16),
        compiler_params=_params(("parallel", "parallel"), blocks),
        name="head_keys",
    )(c, w_uk, kr)


def _head_values_t_kernel(w_ref, c_ref, o_ref):
    o_ref[0, :V_HEAD_DIM, :] = _dot_nt(w_ref[...], c_ref[...]).astype(o_ref.dtype)
    o_ref[0, V_HEAD_DIM:, :] = jnp.ones((ONES_ROWS, o_ref.shape[2]), o_ref.dtype)


def head_values_t(w_uv_t, c, *, tn=1024):
    n, k = c.shape
    tn = _tile(n, tn)
    rows = V_HEAD_DIM + ONES_ROWS
    blocks = _nbytes((V_HEAD_DIM, k), BF16) + _nbytes((tn, k), BF16) + _nbytes((rows, tn), BF16)
    return pl.pallas_call(
        _head_values_t_kernel,
        grid=(N_B_HEADS, n // tn),
        in_specs=[pl.BlockSpec((V_HEAD_DIM, k), lambda h, j: (h, 0)), pl.BlockSpec((tn, k), lambda h, j: (j, 0))],
        out_specs=pl.BlockSpec((1, rows, tn), lambda h, j: (h, 0, j)),
        out_shape=jax.ShapeDtypeStruct((N_B_HEADS, rows, n), BF16),
        compiler_params=_params(("parallel", "parallel"), blocks),
        name="head_values_t",
    )(w_uv_t, c)


def _mla_prompt_kernel(q_ref, kcat_ref, vt1_ref, o_ref, *, tq, hps):
    qi = pl.program_id(1)
    nv = V_HEAD_DIM
    qs = [q_ref[:, a * Q_HEAD_PAD:(a + 1) * Q_HEAD_PAD] for a in range(hps)]

    def step(kb, carry, masked):
        start = pl.multiple_of(kb * tq, tq)
        scores = [_dot_nt(kcat_ref[a, pl.ds(start, tq), :], qs[a]) for a in range(hps)]
        out = []
        for a in range(hps):
            m, acc = carry[a]
            s = scores[a]
            if masked:
                key = lax.broadcasted_iota(jnp.int32, (tq, tq), 0)
                qry = lax.broadcasted_iota(jnp.int32, (tq, tq), 1)
                s = jnp.where(key <= qry, s, NEG)
            m_new = jnp.maximum(m, jnp.max(s, axis=0, keepdims=True))
            alpha = jnp.exp(m - m_new)
            p = jnp.exp(s - m_new).astype(BF16)
            acc = alpha * acc + _dot(vt1_ref[a, :, pl.ds(start, tq)], p)
            out.append((m_new, acc))
        return tuple(out)

    init = tuple((jnp.full((1, tq), NEG, F32), jnp.zeros((vt1_ref.shape[1], tq), F32)) for _ in range(hps))
    carry = lax.fori_loop(0, qi, lambda kb, c: step(kb, c, False), init)
    carry = step(qi, carry, True)
    for a, (_, acc) in enumerate(carry):
        o_ref[:, a * nv:(a + 1) * nv] = (acc[:nv] / acc[nv:nv + 1]).T.astype(o_ref.dtype)


def mla_prompt(q, kcat, vt1, *, tq=512, heads_per_step=4):
    s_len = q.shape[0]
    tq = _tile(s_len, tq)
    hps = heads_per_step
    vrows = vt1.shape[1]
    blocks = _nbytes((tq, hps * Q_HEAD_PAD), BF16) + _nbytes((tq, hps * V_HEAD_DIM), BF16)
    resident = (hps * _nbytes((s_len, Q_HEAD_PAD), BF16) + hps * _nbytes((vrows, s_len), BF16)) // 2
    return pl.pallas_call(
        functools.partial(_mla_prompt_kernel, tq=tq, hps=hps),
        grid=(N_B_HEADS // hps, s_len // tq),
        in_specs=[pl.BlockSpec((tq, hps * Q_HEAD_PAD), lambda h, i: (i, h)),
                  pl.BlockSpec((hps, s_len, Q_HEAD_PAD), lambda h, i: (h, 0, 0), pipeline_mode=pl.Buffered(1)),
                  pl.BlockSpec((hps, vrows, s_len), lambda h, i: (h, 0, 0), pipeline_mode=pl.Buffered(1))],
        out_specs=pl.BlockSpec((tq, hps * V_HEAD_DIM), lambda h, i: (i, h)),
        out_shape=jax.ShapeDtypeStruct((s_len, N_B_HEADS * V_HEAD_DIM), BF16),
        compiler_params=_params(("parallel", "arbitrary"), blocks + 4 * hps * _nbytes((tq, tq), F32), resident),
        name="mla_prompt",
    )(q, kcat, vt1)


def _row_scores(q, k, bias):
    return jnp.sum(k * q[None], axis=-1, keepdims=True) * ATTN_SCALE + bias


def _dswa_sample_kernel(buf_ref, new_ref, q_ref, bias_a_ref, bias_b_ref, nbuf_ref, o_ref, lse_ref, *, d):
    nsub = buf_ref.shape[1]
    nh = HEADS_PER_GROUP
    t_new = new_ref.shape[1]
    for r_out in range(d):
        sh, r_in = divmod(r_out + t_new, d)
        if nsub - sh > 0:
            nbuf_ref[0, 0:nsub - sh, r_out] = buf_ref[0, sh:nsub, r_in]
        for a in range(max(nsub - sh, 0), nsub):
            nbuf_ref[0, a, r_out] = new_ref[0, a * d + r_out + t_new - nsub * d]
    k_new = new_ref[0, :, 0:nh, :]
    v_new = new_ref[0, :, nh:2 * nh, :]
    for i in range(t_new):
        q = q_ref[0, i]
        k_old = buf_ref[0, :, i % d, 0:nh, :]
        v_old = buf_ref[0, :, i % d, nh:2 * nh, :]
        s_a = _row_scores(q, k_old, bias_a_ref[i])
        s_b = _row_scores(q, k_new, bias_b_ref[i])
        m = jnp.maximum(jnp.max(s_a, axis=0), jnp.max(s_b, axis=0))
        e_a = jnp.exp(s_a - m[None])
        e_b = jnp.exp(s_b - m[None])
        l = jnp.sum(e_a, axis=0) + jnp.sum(e_b, axis=0)
        acc = jnp.sum(e_a * v_old, axis=0) + jnp.sum(e_b * v_new, axis=0)
        o_ref[0, i] = acc / l
        lse_ref[0, i] = m + jnp.log(l)


def dswa_sample(buf, new_kv, q, bias_a, bias_b, d):
    b, l_buf = buf.shape[:2]
    t_new = new_kv.shape[1]
    nsub = l_buf // d
    rows = 2 * HEADS_PER_GROUP
    bufv = buf.reshape(b, nsub, d, rows, HEAD_DIM)
    buf_spec = pl.BlockSpec((1, nsub, d, rows, HEAD_DIM), lambda i: (i, 0, 0, 0, 0))
    tok = lambda r: pl.BlockSpec((1, t_new, r, HEAD_DIM), lambda i: (i, 0, 0, 0))
    full = lambda a: pl.BlockSpec(a.shape, lambda i: (0,) * a.ndim)
    blocks = 2 * _nbytes((l_buf, rows, HEAD_DIM), F32) + _nbytes(bias_a.shape, F32)
    nbuf, o, lse = pl.pallas_call(
        functools.partial(_dswa_sample_kernel, d=d),
        grid=(b,),
        in_specs=[buf_spec, tok(rows), tok(HEADS_PER_GROUP), full(bias_a), full(bias_b)],
        out_specs=[buf_spec, tok(HEADS_PER_GROUP), tok(HEADS_PER_GROUP)],
        out_shape=[jax.ShapeDtypeStruct(bufv.shape, F32),
                   jax.ShapeDtypeStruct((b, t_new, HEADS_PER_GROUP, HEAD_DIM), F32),
                   jax.ShapeDtypeStruct((b, t_new, HEADS_PER_GROUP, HEAD_DIM), F32)],
        compiler_params=_params(("parallel",), blocks),
        name=f"dswa_sample_d{d}",
    )(bufv, new_kv, q, bias_a, bias_b)
    return nbuf.reshape(buf.shape), o, lse


def _dswa_sample_dense_kernel(buf_ref, new_ref, q_ref, bias_ref, nbuf_ref, o_ref, lse_ref):
    nh = HEADS_PER_GROUP
    l_buf, rows = buf_ref.shape[1], buf_ref.shape[2]
    t_new = new_ref.shape[1]
    ncol = (l_buf + t_new) * rows
    data, scores = [], []
    for j in range(q_ref.shape[0]):
        x, fresh = buf_ref[j], new_ref[j]
        nbuf_ref[j, 0:l_buf - t_new] = x[t_new:]
        nbuf_ref[j, l_buf - t_new:] = fresh
        src = jnp.concatenate([x, fresh], axis=0)
        swapped = pltpu.roll(src, nh, axis=1).reshape(ncol, HEAD_DIM).astype(BF16)
        data.append(src.reshape(ncol, HEAD_DIM).astype(BF16))
        scores.append(_dot_nt(q_ref[j].astype(BF16), swapped) * ATTN_SCALE + bias_ref[...])
    for j, s in enumerate(scores):
        m = jnp.max(s, axis=-1, keepdims=True)
        e = jnp.exp(s - m)
        l = jnp.sum(e, axis=-1, keepdims=True)
        o_ref[j] = _dot(e.astype(BF16), data[j]) / l
        lse_ref[j] = jnp.broadcast_to(m + jnp.log(l), o_ref.shape[1:])


def dswa_sample_dense(buf, new_kv, q, bias):
    b, l_buf, rows, _ = buf.shape
    t_new = new_kv.shape[1]
    seq_bytes = _nbytes((l_buf, rows, HEAD_DIM), F32)
    nb = max(1, min(8, (2 << 20) // seq_bytes))
    nb = nb if b % nb == 0 else 1
    lead = lambda a: pl.BlockSpec((nb,) + a.shape[1:], lambda i: (i,) + (0,) * (a.ndim - 1))
    blocks = 2 * nb * seq_bytes + _nbytes(bias.shape, F32)
    return pl.pallas_call(
        _dswa_sample_dense_kernel,
        grid=(b // nb,),
        in_specs=[lead(buf), lead(new_kv), lead(q), pl.BlockSpec(bias.shape, lambda i: (0, 0))],
        out_specs=[lead(buf), lead(q), lead(q)],
        out_shape=[jax.ShapeDtypeStruct(buf.shape, F32), jax.ShapeDtypeStruct(q.shape, F32),
                   jax.ShapeDtypeStruct(q.shape, F32)],
        compiler_params=_params(("parallel",), blocks, 2 * nb * seq_bytes),
        name=f"dswa_sample_dense_l{l_buf}",
    )(buf, new_kv, q, bias)


def _mem_attn_sample_kernel(kv_ref, q_ref, o_ref):
    nh = N_MEM_HEADS
    nq = q_ref.shape[1]
    n_mem, rows = kv_ref.shape[2], kv_ref.shape[3]
    ncol = n_mem * rows
    col_row = lax.broadcasted_iota(jnp.int32, (nq, ncol), 1) % rows
    q_head = lax.broadcasted_iota(jnp.int32, (nq, ncol), 0) // (nq // nh)
    own = col_row == q_head + nh
    data, scores = [], []
    for j in range(q_ref.shape[0]):
        x = kv_ref[0, j]
        swapped = pltpu.roll(x, nh, axis=1).reshape(ncol, HEAD_DIM).astype(BF16)
        data.append(x.reshape(ncol, HEAD_DIM).astype(BF16))
        scores.append(_dot_nt(q_ref[j].astype(BF16), swapped) * ATTN_SCALE)
    for j, s in enumerate(scores):
        s = jnp.where(own, s, NEG)
        e = jnp.exp(s - jnp.max(s, axis=-1, keepdims=True))
        o_ref[j] = _dot(e.astype(BF16), data[j]) / jnp.sum(e, axis=-1, keepdims=True)


def mem_attn_sample(cache, layer, q, *, seqs_per_step=4):
    _, b, n_mem, rows, _ = cache.shape
    nb = seqs_per_step if b % seqs_per_step == 0 else 1
    tok = pl.BlockSpec((nb,) + q.shape[1:], lambda i: (i, 0, 0))
    return pl.pallas_call(
        _mem_attn_sample_kernel,
        grid=(b // nb,),
        in_specs=[pl.BlockSpec((1, nb, n_mem, rows, HEAD_DIM), lambda i: (layer, i, 0, 0, 0)), tok],
        out_specs=tok,
        out_shape=jax.ShapeDtypeStruct(q.shape, F32),
        compiler_params=_params(("parallel",), nb * _nbytes((n_mem, rows, HEAD_DIM), F32),
                                2 * nb * _nbytes((n_mem, rows, HEAD_DIM), F32)),
        name="mem_attn_sample",
    )(cache, q)


def _mla_sample_kernel(pt_ref, q_ref, new_ref, *rest, n_pages, n_seq, t_new):
    del pt_ref
    page_refs = rest[:n_pages * n_seq]
    o_ref, m_ref, l_ref, acc_ref = rest[n_pages * n_seq:]
    c = pl.program_id(1)
    nq = q_ref.shape[1]

    @pl.when(c == 0)
    def _():
        m_ref[...] = jnp.full(m_ref.shape, NEG, F32)
        l_ref[...] = jnp.zeros(l_ref.shape, F32)
        acc_ref[...] = jnp.zeros(acc_ref.shape, F32)

    def update(j, s_parts, pv_fn):
        m_prev = m_ref[j]
        m_cur = jnp.max(functools.reduce(jnp.maximum, s_parts), axis=-1, keepdims=True)
        m_new = jnp.maximum(m_prev, m_cur)
        alpha = jnp.exp(m_prev - m_new)
        p = [jnp.exp(s - m_new) for s in s_parts]
        l_ref[j] = alpha * l_ref[j] + jnp.sum(functools.reduce(jnp.add, p), axis=-1, keepdims=True)
        m_ref[j] = m_new
        acc_ref[j] = alpha * acc_ref[j] + pv_fn([x.astype(BF16) for x in p])

    pages = [[r[0].astype(BF16) for r in page_refs[j * n_pages:(j + 1) * n_pages]] for j in range(n_seq)]
    scores = [[_dot(q_ref[j], kt) for kt in pages[j]] for j in range(n_seq)]
    for j in range(n_seq):
        update(j, scores[j], lambda p, kts=pages[j]: functools.reduce(
            jnp.add, [_dot_nt(pi, kt[:KV_LORA_RANK]) for pi, kt in zip(p, kts)]))

    @pl.when(c == pl.num_programs(1) - 1)
    def _():
        tok = lax.broadcasted_iota(jnp.int32, (nq, t_new), 0) // (nq // t_new)
        key = lax.broadcasted_iota(jnp.int32, (nq, t_new), 1)
        for j in range(n_seq):
            nr = new_ref[j].astype(BF16)
            s = jnp.where(key <= tok, _dot_nt(q_ref[j], nr), NEG)
            update(j, [s], lambda p, nr=nr: _dot(p[0], nr[:, :KV_LORA_RANK]))
            o_ref[j] = acc_ref[j] / l_ref[j]


def mla_sample(q, new_rows, cache_t, page_table):
    b, nq, row = q.shape
    t_new = new_rows.shape[1]
    n_seq_pages = page_table.shape[1]
    pps = PAGES_PER_STEP if n_seq_pages % PAGES_PER_STEP == 0 else 1
    nb = SEQS_PER_STEP if b % SEQS_PER_STEP == 0 else 1

    def page_spec(j, p):
        return pl.BlockSpec((1, row, PAGE_SIZE), lambda i, c, pt: (pt[i * nb + j, c * pps + p], 0, 0))

    blocks = (nb * pps * _nbytes((row, PAGE_SIZE), F32) + nb * _nbytes((nq, row), BF16)
              + nb * _nbytes((nq, KV_LORA_RANK), F32))
    return pl.pallas_call(
        functools.partial(_mla_sample_kernel, n_pages=pps, n_seq=nb, t_new=t_new),
        grid_spec=pltpu.PrefetchScalarGridSpec(
            num_scalar_prefetch=1,
            grid=(b // nb, n_seq_pages // pps),
            in_specs=[pl.BlockSpec((nb, nq, row), lambda i, c, pt: (i, 0, 0)),
                      pl.BlockSpec((nb, t_new, row), lambda i, c, pt: (i, 0, 0))]
                     + [page_spec(j, p) for j in range(nb) for p in range(pps)],
            out_specs=pl.BlockSpec((nb, nq, KV_LORA_RANK), lambda i, c, pt: (i, 0, 0)),
            scratch_shapes=[pltpu.VMEM((nb, nq, 1), F32), pltpu.VMEM((nb, nq, 1), F32),
                            pltpu.VMEM((nb, nq, KV_LORA_RANK), F32)]),
        out_shape=jax.ShapeDtypeStruct((b, nq, KV_LORA_RANK), F32),
        compiler_params=_params(("parallel", "arbitrary"), blocks,
                                nb * pps * _nbytes((row, PAGE_SIZE), BF16)),
        name="mla_sample",
    )(page_table, q, new_rows, *([cache_t] * (nb * pps)))


def _t5_bucket(dist):
    max_exact = N_BUCKETS // 2
    dd = jnp.maximum(dist, 1).astype(F32)
    large = max_exact + (jnp.log(dd / max_exact) / math.log(MAX_DISTANCE / max_exact)
                         * (N_BUCKETS - max_exact)).astype(jnp.int32)
    large = jnp.minimum(large, N_BUCKETS - 1)
    return jnp.where(dist < max_exact, dist, large)


def _group_bias(t5_bias, g):
    w, d = DSWA_GROUPS[g]
    offs = jnp.arange(w // d + 1, dtype=jnp.int32) * d
    return t5_bias[_t5_bucket(offs), g * HEADS_PER_GROUP:(g + 1) * HEADS_PER_GROUP].astype(F32)


def _banded(table, j):
    jmax = table.shape[0] - 1
    vals = table[jnp.clip(j, 0, jmax)]
    return jnp.where(((j >= 0) & (j <= jmax))[..., None], vals, NEG)


def _prompt_bias(table):
    blk, nh = table.shape[0] - 1, table.shape[1]
    n = 3 * blk - 1
    pad = jnp.full((blk - 1, nh), NEG, F32)
    u = jnp.concatenate([pad, table, pad], axis=0)
    skew = jnp.tile(u, (blk + 1, 1))[:blk * (n + 1)].reshape(blk, n + 1, nh)
    return jnp.moveaxis(skew[:, :2 * blk][:, ::-1], -1, 0)


def _sample_bias(table, d, t_new):
    blk = table.shape[0] - 1
    i = jnp.arange(t_new)
    j_a = blk - jnp.arange(blk)[None, :] + (i // d)[:, None]
    diff = i[:, None] - i[None, :]
    j_b = jnp.where((diff >= 0) & (diff % d == 0), diff // d, -1)
    lanes = lambda x: jnp.broadcast_to(x[..., None], x.shape + (HEAD_DIM,))
    return lanes(_banded(table, j_a)), lanes(_banded(table, j_b))


def _sample_bias_dense(table, d, t_new):
    blk, nh = table.shape[0] - 1, table.shape[1]
    w = blk * d
    stuffed = jnp.concatenate([table[:, None, :], jnp.full((blk + 1, d - 1, nh), NEG, F32)], axis=1)
    stuffed = stuffed.reshape((blk + 1) * d, nh)[:w + 1]
    pad = jnp.full((t_new - 1, nh), NEG, F32)
    rev = jnp.concatenate([pad, stuffed, pad], axis=0)[::-1]
    band = jnp.stack([rev[t_new - 1 - i:t_new - 1 - i + w + t_new] for i in range(t_new)])
    band = jnp.transpose(band, (2, 0, 1))[..., None]
    head = jnp.arange(nh)[:, None, None, None]
    sub = jnp.arange(2 * nh)[None, None, None, :]
    return jnp.where(sub == head + nh, band, NEG).reshape(nh * t_new, (w + t_new) * 2 * nh)


def _rope_tables(pos):
    half = QK_ROPE_DIM // 2
    inv_freq = ROPE_THETA ** (-jnp.arange(half, dtype=F32) / half)
    ang = pos.astype(F32)[:, None] * inv_freq[None, :]
    cos, sin = jnp.cos(ang), jnp.sin(ang)
    zero = jnp.zeros_like(cos)
    cc = jnp.concatenate([cos, cos, zero, zero], axis=-1)
    shi = jnp.concatenate([zero, sin, zero, zero], axis=-1)
    slo = jnp.concatenate([-sin, zero, zero, zero], axis=-1)
    return cc, shi, slo


def _pad_last(w, to):
    return jnp.pad(w, [(0, 0)] * (w.ndim - 1) + [(0, to - w.shape[-1])])


def _trunk(x, rope_tables, dswa_fn, mem_fn, mla_fn, p):
    n = x.shape[0]
    z0 = norm_matmul(x, p['g_attn_pre'][0], p['w_a_in'])
    o_tok = dswa_fn(z0)
    o_mem = mem_fn(0, z0, A_QKV_WIDTH // MEM_WIDTH)
    x = out_proj(o_tok, o_mem, p['w_a_out'][:GROUP_WIDTH], p['w_a_out'][GROUP_WIDTH:], p['g_attn_post'][0], x)
    x = mlp(x, p['g_mlp_pre'][0], p['w_mlp_up'][0], p['w_mlp_down'][0], p['g_mlp_post'][0])
    raw = norm_matmul(x, p['g_kv_in'], p['w_kv_down'])
    rows, c_bf, kr_bf = kv_finish(raw, p['g_kv_latent'], rope_tables)
    z = norm_matmul(x, p['g_attn_pre'][1], p['w_b_in'])
    q = norm_matmul(z, p['g_q_latent'], p['w_q_up'], k_width=Q_LORA_RANK, rope_tables=rope_tables,
                    scale=MLA_SCALE, out_dtype=BF16, tn=2 * Q_HEAD_PAD)
    o_tok = mla_fn(q, rows, c_bf, kr_bf)
    o_mem = mem_fn(1, z, Q_LORA_RANK // MEM_WIDTH)
    nv = N_B_HEADS * V_HEAD_DIM
    x = out_proj(o_tok, o_mem, p['w_b_out'][:nv], p['w_b_out'][nv:], p['g_attn_post'][1], x)
    x = mlp(x, p['g_mlp_pre'][1], p['w_mlp_up'][1], p['w_mlp_down'][1], p['g_mlp_post'][1])
    return x, rows[:, :MLA_ROW], z0


def kernel(x_prompt, x_sample, mem_prompt, cache_swa_kv_w128, cache_swa_kv_w512, cache_swa_kv_w2048, cache_mla_kv, cache_mem_kv, page_table, t5_bias, g_attn_pre, g_attn_post, g_mlp_pre, g_mlp_post, g_mem, w_mem_kv, w_mlp_up, w_mlp_down, w_a_in, w_a_out, g_kv_in, w_kv_down, g_kv_latent, w_kv_up, w_b_in, g_q_latent, w_q_up, w_b_out):
    depth = g_attn_pre.shape[0]
    assert depth == 2 and w_a_in.shape[0] == 1 and w_b_in.shape[0] == 1
    bp, s_len, d_model = x_prompt.shape
    assert bp == 1
    bs, t_new, _ = x_sample.shape
    past = page_table.shape[1] * PAGE_SIZE
    bufs = (cache_swa_kv_w128, cache_swa_kv_w512, cache_swa_kv_w2048)
    for buf, (w, d) in zip(bufs, DSWA_GROUPS):
        assert buf.shape[2] == w and past >= w and s_len % (d * (w // d)) == 0 and s_len >= w

    w_q_up_pad = _pad_last(w_q_up[0].reshape(Q_LORA_RANK, N_B_HEADS, QK_NOPE_DIM + QK_ROPE_DIM),
                           Q_HEAD_PAD).reshape(Q_LORA_RANK, N_B_HEADS * Q_HEAD_PAD)
    p = {
        'g_attn_pre': g_attn_pre, 'g_attn_post': g_attn_post, 'g_mlp_pre': g_mlp_pre, 'g_mlp_post': g_mlp_post,
        'g_kv_in': g_kv_in, 'g_kv_latent': g_kv_latent, 'g_q_latent': g_q_latent[0],
        'w_a_in': w_a_in[0].astype(BF16), 'w_a_out': w_a_out[0].astype(BF16),
        'w_b_in': w_b_in[0].astype(BF16), 'w_b_out': w_b_out[0].astype(BF16),
        'w_mlp_up': w_mlp_up.astype(BF16), 'w_mlp_down': w_mlp_down.astype(BF16),
        'w_kv_down': _pad_last(w_kv_down, MLA_ROW_PAD).astype(BF16),
        'w_q_up': w_q_up_pad.astype(BF16),
    }
    w_uk = w_kv_up[..., :QK_NOPE_DIM]
    w_uv = w_kv_up[..., QK_NOPE_DIM:]
    w_uk_cat = w_uk.reshape(KV_LORA_RANK, -1).astype(BF16)
    w_uv_cat_t = w_uv.reshape(KV_LORA_RANK, -1).T.astype(BF16)
    w_uk_t =jnp.transpose(w_uk, (1, 2, 0)).astype(BF16)
    w_uv_h = jnp.transpose(w_uv, (1, 0, 2)).astype(BF16)
    tables = [_group_bias(t5_bias, g) for g in range(N_GROUPS)]

    mem_kv_prompt = [norm_matmul(mem_prompt[0], g_mem[l], w_mem_kv[l].astype(BF16)) for l in range(depth)]

    def dswa_p(z):
        parts = [dswa_prompt(z, g, _prompt_bias(tables[g])) for g in range(N_GROUPS)]
        return combine_groups([o for o, _ in parts], [l for _, l in parts])

    def mla_p(q, rows, c_bf, kr_bf):
        return mla_prompt(q, head_keys(c_bf, w_uk_cat, kr_bf), head_values_t(w_uv_cat_t, c_bf))

    y_p, rows_p, z_p = _trunk(x_prompt[0], _rope_tables(jnp.arange(s_len, dtype=jnp.int32)), dswa_p,
                              lambda l, z, cb: mem_attn_prompt(z, cb, mem_kv_prompt[l]), mla_p, p)
    swa_p = []
    for g, (w, _) in enumerate(DSWA_GROUPS):
        k_g = z_p[s_len - w:, (N_GROUPS + g) * GROUP_WIDTH:(N_GROUPS + g + 1) * GROUP_WIDTH]
        v_g = z_p[s_len - w:, (2 * N_GROUPS + g) * GROUP_WIDTH:(2 * N_GROUPS + g + 1) * GROUP_WIDTH]
        kv_g = jnp.stack([k_g.reshape(w, HEADS_PER_GROUP, HEAD_DIM), v_g.reshape(w, HEADS_PER_GROUP, HEAD_DIM)], axis=1)
        swa_p.append(kv_g[None, None])
    mem_kv_out = jnp.stack(mem_kv_prompt).reshape(depth, 1, mem_prompt.shape[1], 2, N_MEM_HEADS, HEAD_DIM)

    ns = bs * t_new
    pos_s = past + (jnp.arange(ns, dtype=jnp.int32) % t_new)
    swa_s = []

    def heads(a, nh):
        return a.reshape(bs, t_new, nh, HEAD_DIM)

    def dswa_s(z):
        outs, lses = [], []
        for g, (w, d) in enumerate(DSWA_GROUPS):
            sec = lambda c: z[:, (c * N_GROUPS + g) * GROUP_WIDTH:(c * N_GROUPS + g + 1) * GROUP_WIDTH]
            new_kv = jnp.concatenate([heads(sec(1), HEADS_PER_GROUP), heads(sec(2), HEADS_PER_GROUP)], axis=2)
            buf = bufs[g][0].reshape(bs, w, 2 * HEADS_PER_GROUP, HEAD_DIM)
            q = heads(sec(0), HEADS_PER_GROUP)
            if w <= DENSE_SAMPLE_MAX_ROWS:
                q = jnp.swapaxes(q, 1, 2).reshape(bs, HEADS_PER_GROUP * t_new, HEAD_DIM)
                nbuf, o, lse = dswa_sample_dense(buf, new_kv, q, _sample_bias_dense(tables[g], d, t_new))
                o, lse = (jnp.swapaxes(a.reshape(bs, HEADS_PER_GROUP, t_new, HEAD_DIM), 1, 2) for a in (o, lse))
            else:
                bias_a, bias_b = _sample_bias(tables[g], d, t_new)
                nbuf, o, lse = dswa_sample(buf, new_kv, q, bias_a, bias_b, d)
            swa_s.append(nbuf.reshape(bufs[g].shape))
            outs.append(o.reshape(ns, GROUP_WIDTH))
            lses.append(lse.reshape(ns, GROUP_WIDTH))
        return combine_groups(outs, lses)

    cache_mem = cache_mem_kv.reshape(depth, bs, cache_mem_kv.shape[2], 2 * N_MEM_HEADS, HEAD_DIM)

    def mem_s(l, z, cb):
        q = heads(z[:, cb * MEM_WIDTH:(cb + 1) * MEM_WIDTH], N_MEM_HEADS)
        q = jnp.swapaxes(q, 1, 2).reshape(bs, N_MEM_HEADS * t_new, HEAD_DIM)
        o = mem_attn_sample(cache_mem, l, q).reshape(bs, N_MEM_HEADS, t_new, HEAD_DIM)
        return jnp.swapaxes(o, 1, 2).reshape(ns, MEM_WIDTH)

    cache_mla_t = jnp.swapaxes(cache_mla_kv, 1, 2)

    def mla_s(q, rows, c_bf, kr_bf):
        q_lat = head_matmul(q, w_uk_t, x_block_stride=Q_HEAD_PAD // QK_NOPE_DIM, out_dtype=BF16)
        q3 = q.reshape(ns, N_B_HEADS, Q_HEAD_PAD)
        q_full = jnp.concatenate([q_lat.reshape(ns, N_B_HEADS, KV_LORA_RANK),
                                  q3[:, :, QK_NOPE_DIM:QK_NOPE_DIM + QK_ROPE_DIM]], axis=-1)
        q_full = q_full.reshape(bs, t_new * N_B_HEADS, MLA_ROW)
        new_rows = rows[:, :MLA_ROW].reshape(bs, t_new, MLA_ROW)
        o_lat = mla_sample(q_full, new_rows, cache_mla_t, page_table)
        return head_matmul(o_lat.reshape(ns, N_B_HEADS * KV_LORA_RANK), w_uv_h, out_dtype=BF16)

    y_s, rows_s, _ = _trunk(x_sample.reshape(ns, d_model), _rope_tables(pos_s), dswa_s, mem_s, mla_s, p)

    return (y_p[None], y_s.reshape(bs, t_new, d_model), swa_p[0], swa_p[1], swa_p[2],
            rows_p[None], mem_kv_out, swa_s[0], swa_s[1], swa_s[2], rows_s.reshape(bs, t_new, MLA_ROW))
```

```python
import functools
import math

import jax
import jax.numpy as jnp
from jax import lax
from jax.experimental import pallas as pl
from jax.experimental.pallas import tpu as pltpu

F32 = jnp.float32
BF16 = jnp.bfloat16

HEAD_DIM = 128
DSWA_GROUPS = ((128, 1), (512, 4), (2048, 16))
N_GROUPS = len(DSWA_GROUPS)
HEADS_PER_GROUP = 4
GROUP_WIDTH = HEADS_PER_GROUP * HEAD_DIM
N_A_HEADS = N_GROUPS * HEADS_PER_GROUP
A_QKV_WIDTH = 3 * N_A_HEADS * HEAD_DIM
N_MEM_HEADS = 4
MEM_WIDTH = N_MEM_HEADS * HEAD_DIM
N_B_HEADS = 12
Q_LORA_RANK = 1536
KV_LORA_RANK = 512
QK_NOPE_DIM = 128
QK_ROPE_DIM = 64
V_HEAD_DIM = 128
MLA_ROW = KV_LORA_RANK + QK_ROPE_DIM
MLA_SCALE = (QK_NOPE_DIM + QK_ROPE_DIM) ** -0.5
ATTN_SCALE = HEAD_DIM ** -0.5
ROPE_THETA = 10000.0
N_BUCKETS = 32
MAX_DISTANCE = 2048
EPS = 1e-6
PAGE_SIZE = 128

LANES = 128
ROPE_PAD = LANES
Q_HEAD_PAD = QK_NOPE_DIM + ROPE_PAD
MLA_ROW_PAD = KV_LORA_RANK + ROPE_PAD
NEG = -1e30
VMEM_BYTES_V7X = 64 * 1024 * 1024
VMEM_CAP = VMEM_BYTES_V7X - 8 * 1024 * 1024
PAGES_PER_STEP = 8
SEQS_PER_STEP = 2
DENSE_SAMPLE_MAX_ROWS = 512


def _tile(n, pref):
    return pref if n % pref == 0 else n


def _nbytes(shape, dtype):
    return math.prod(shape) * jnp.dtype(dtype).itemsize


def _params(semantics, pipelined_bytes, resident_bytes=0):
    est = 2 * pipelined_bytes + 2 * resident_bytes + (4 << 20)
    return pltpu.CompilerParams(dimension_semantics=semantics,
                                vmem_limit_bytes=int(min(max(est, 16 << 20), VMEM_CAP)))


def _rms(x, g):
    return x * lax.rsqrt(jnp.mean(x * x, axis=-1, keepdims=True) + EPS) * g


def _rope_pad(r, cc, shi, slo):
    half = QK_ROPE_DIM // 2
    return r * cc + pltpu.roll(r, half, axis=1) * shi + pltpu.roll(r, LANES - half, axis=1) * slo


def _dot(a, b):
    return jnp.dot(a, b, preferred_element_type=F32)


def _dot_nt(a, b):
    return lax.dot_general(a, b, (((1,), (1,)), ((), ())), preferred_element_type=F32)


def _norm_matmul_kernel(*refs, norm, rope, scale):
    if rope:
        x_ref, g_ref, w_ref, cc_ref, shi_ref, slo_ref, o_ref, xn_ref = refs
    else:
        x_ref, g_ref, w_ref, o_ref, xn_ref = refs

    @pl.when(pl.program_id(1) == 0)
    def _():
        x = x_ref[...].astype(F32)
        if norm:
            x = _rms(x, g_ref[...])
        xn_ref[...] = x.astype(BF16)

    acc = _dot(xn_ref[...], w_ref[...])
    if scale != 1.0:
        acc = acc * scale
    if rope:
        cc, shi, slo = cc_ref[...], shi_ref[...], slo_ref[...]
        for h in range(acc.shape[1] // Q_HEAD_PAD):
            lo = h * Q_HEAD_PAD
            o_ref[:, lo:lo + QK_NOPE_DIM] = acc[:, lo:lo + QK_NOPE_DIM].astype(o_ref.dtype)
            r = _rope_pad(acc[:, lo + QK_NOPE_DIM:lo + Q_HEAD_PAD], cc, shi, slo)
            o_ref[:, lo + QK_NOPE_DIM:lo + Q_HEAD_PAD] = r.astype(o_ref.dtype)
    else:
        o_ref[...] = acc.astype(o_ref.dtype)


def norm_matmul(x, g, w, *, k_width=None, norm=True, rope_tables=None, scale=1.0,
                out_dtype=F32, tm=512, tn=512):
    n = x.shape[0]
    k = k_width or x.shape[1]
    nout = w.shape[1]
    tm = _tile(n, tm)
    tn = _tile(nout, tn)
    rope = rope_tables is not None
    in_specs = [pl.BlockSpec((tm, k), lambda i, j: (i, 0)),
                pl.BlockSpec((1, k), lambda i, j: (0, 0)),
                pl.BlockSpec((k, tn), lambda i, j: (0, j))]
    args = [x, g.reshape(1, k).astype(F32), w]
    if rope:
        in_specs += [pl.BlockSpec((tm, LANES), lambda i, j: (i, 0))] * 3
        args += list(rope_tables)
    blocks = (_nbytes((tm, k), x.dtype) + _nbytes((k, tn), w.dtype) + _nbytes((tm, tn), out_dtype)
              + 3 * _nbytes((tm, LANES), F32))
    return pl.pallas_call(
        functools.partial(_norm_matmul_kernel, norm=norm, rope=rope, scale=scale),
        grid=(n // tm, nout // tn),
        in_specs=in_specs,
        out_specs=pl.BlockSpec((tm, tn), lambda i, j: (i, j)),
        out_shape=jax.ShapeDtypeStruct((n, nout), out_dtype),
        scratch_shapes=[pltpu.VMEM((tm, k), BF16)],
        compiler_params=_params(("parallel", "arbitrary"), blocks,
                                _nbytes((tm, k), F32) + _nbytes((tm, tn), F32)),
        name="norm_matmul",
    )(*args)


def _kv_finish_kernel(raw_ref, g_ref, cc_ref, shi_ref, slo_ref, rows_ref, c_ref, kr_ref):
    c = _rms(raw_ref[:, :KV_LORA_RANK], g_ref[...])
    kr = _rope_pad(raw_ref[:, KV_LORA_RANK:], cc_ref[...], shi_ref[...], slo_ref[...])
    rows_ref[:, :KV_LORA_RANK] = c
    rows_ref[:, KV_LORA_RANK:] = kr
    c_ref[...] = c.astype(BF16)
    kr_ref[...] = kr.astype(BF16)


def kv_finish(raw, g_latent, rope_tables, *, tm=512):
    n = raw.shape[0]
    tm = _tile(n, tm)
    row = lambda w: pl.BlockSpec((tm, w), lambda i: (i, 0))
    blocks = 2 * _nbytes((tm, MLA_ROW_PAD), F32) + 4 * _nbytes((tm, LANES), F32) + _nbytes((tm, KV_LORA_RANK), BF16)
    return pl.pallas_call(
        _kv_finish_kernel,
        grid=(n // tm,),
        in_specs=[row(MLA_ROW_PAD), pl.BlockSpec((1, KV_LORA_RANK), lambda i: (0, 0)),
                  row(LANES), row(LANES), row(LANES)],
        out_specs=[row(MLA_ROW_PAD), row(KV_LORA_RANK), row(LANES)],
        out_shape=[jax.ShapeDtypeStruct((n, MLA_ROW_PAD), F32),
                   jax.ShapeDtypeStruct((n, KV_LORA_RANK), BF16),
                   jax.ShapeDtypeStruct((n, LANES), BF16)],
        compiler_params=_params(("parallel",), blocks),
        name="kv_finish",
    )(raw, g_latent.reshape(1, -1).astype(F32), *rope_tables)


def _out_proj_kernel(a1_ref, a2_ref, w1_ref, w2_ref, g_ref, x_ref, o_ref):
    o = _dot(a1_ref[...].astype(BF16), w1_ref[...]) + _dot(a2_ref[...].astype(BF16), w2_ref[...])
    o_ref[...] = x_ref[...] + _rms(o, g_ref[...])


def out_proj(a1, a2, w1, w2, g, x, *, tm=512):
    n, d = x.shape
    k1, k2 = a1.shape[1], a2.shape[1]
    tm = _tile(n, tm)
    blocks = (_nbytes((tm, k1), a1.dtype) + _nbytes((tm, k2), a2.dtype) + _nbytes((k1 + k2, d), BF16)
              + 2 * _nbytes((tm, d), F32))
    return pl.pallas_call(
        _out_proj_kernel,
        grid=(n // tm,),
        in_specs=[pl.BlockSpec((tm, k1), lambda i: (i, 0)), pl.BlockSpec((tm, k2), lambda i: (i, 0)),
                  pl.BlockSpec((k1, d), lambda i: (0, 0)), pl.BlockSpec((k2, d), lambda i: (0, 0)),
                  pl.BlockSpec((1, d), lambda i: (0, 0)), pl.BlockSpec((tm, d), lambda i: (i, 0))],
        out_specs=pl.BlockSpec((tm, d), lambda i: (i, 0)),
        out_shape=jax.ShapeDtypeStruct((n, d), F32),
        compiler_params=_params(("parallel",), blocks, _nbytes((tm, d), F32)),
        name="out_proj",
    )(a1, a2, w1, w2, g.reshape(1, d).astype(F32), x)


def _mlp_kernel(x_ref, gpre_ref, wup_ref, wdn_ref, gpost_ref, o_ref, xn_ref, acc_ref):
    j = pl.program_id(1)

    @pl.when(j == 0)
    def _():
        xn_ref[...] = _rms(x_ref[...], gpre_ref[...]).astype(BF16)

    h = _dot(xn_ref[...], wup_ref[...])
    u = jnp.square(jnp.maximum(h, 0.0)).astype(BF16)
    part = _dot(u, wdn_ref[...])

    @pl.when(j == 0)
    def _():
        acc_ref[...] = part

    @pl.when(j > 0)
    def _():
        acc_ref[...] += part

    @pl.when(j == pl.num_programs(1) - 1)
    def _():
        o_ref[...] = x_ref[...] + _rms(acc_ref[...], gpost_ref[...])


def mlp(x, g_pre, w_up, w_down, g_post, *, tm=512, tf=1024):
    n, d = x.shape
    f = w_up.shape[1]
    tm = _tile(n, tm)
    tf = _tile(f, tf)
    blocks = 2 * _nbytes((tm, d), F32) + 2 * _nbytes((d, tf), BF16)
    resident = _nbytes((tm, d), BF16) + _nbytes((tm, d), F32) + _nbytes((tm, tf), F32)
    return pl.pallas_call(
        _mlp_kernel,
        grid=(n // tm, f // tf),
        in_specs=[pl.BlockSpec((tm, d), lambda i, j: (i, 0)), pl.BlockSpec((1, d), lambda i, j: (0, 0)),
                  pl.BlockSpec((d, tf), lambda i, j: (0, j)), pl.BlockSpec((tf, d), lambda i, j: (j, 0)),
                  pl.BlockSpec((1, d), lambda i, j: (0, 0))],
        out_specs=pl.BlockSpec((tm, d), lambda i, j: (i, 0)),
        out_shape=jax.ShapeDtypeStruct((n, d), F32),
        scratch_shapes=[pltpu.VMEM((tm, d), BF16), pltpu.VMEM((tm, d), F32)],
        compiler_params=_params(("parallel", "arbitrary"), blocks, resident),
        name="mlp",
    )(x, g_pre.reshape(1, d).astype(F32), w_up, w_down, g_post.reshape(1, d).astype(F32))


def _head_matmul_kernel(x_ref, w_ref, o_ref):
    o_ref[...] = _dot(x_ref[...].astype(BF16), w_ref[0]).astype(o_ref.dtype)


def head_matmul(x, w, *, x_block_stride=1, out_dtype=F32):
    n = x.shape[0]
    nh, kh, nn = w.shape
    blocks = _nbytes((n, kh), x.dtype) + _nbytes((kh, nn), w.dtype) + _nbytes((n, nn), out_dtype)
    return pl.pallas_call(
        _head_matmul_kernel,
        grid=(nh,),
        in_specs=[pl.BlockSpec((n, kh), lambda h: (0, h * x_block_stride)),
                  pl.BlockSpec((1, kh, nn), lambda h: (h, 0, 0))],
        out_specs=pl.BlockSpec((n, nn), lambda h: (0, h)),
        out_shape=jax.ShapeDtypeStruct((n, nh * nn), out_dtype),
        compiler_params=_params(("parallel",), blocks),
        name="head_matmul",
    )(x, w)


def _combine_kernel(o0, o1, o2, l0, l1, l2, out_ref):
    m = jnp.maximum(jnp.maximum(l0[...], l1[...]), l2[...])
    w0, w1, w2 = jnp.exp(l0[...] - m), jnp.exp(l1[...] - m), jnp.exp(l2[...] - m)
    out_ref[...] = (w0 * o0[...] + w1 * o1[...] + w2 * o2[...]) / (w0 + w1 + w2)


def combine_groups(outs, lses, *, tm=512):
    n, w = outs[0].shape
    tm = _tile(n, tm)
    spec = pl.BlockSpec((tm, w), lambda i: (i, 0))
    return pl.pallas_call(
        _combine_kernel,
        grid=(n // tm,),
        in_specs=[spec] * 6,
        out_specs=spec,
        out_shape=jax.ShapeDtypeStruct((n, w), F32),
        compiler_params=_params(("parallel",), 7 * _nbytes((tm, w), F32)),
        name="combine_groups",
    )(*outs, *lses)


def _dswa_prompt_kernel(q_ref, kp_ref, kc_ref, vp_ref, vc_ref, bias_ref, o_ref, lse_ref):
    blk = q_ref.shape[0]
    col = lax.broadcasted_iota(jnp.int32, (blk, 2 * blk), 1)
    prev_invalid = col < jnp.where(pl.program_id(1) == 0, blk, 0)
    for h in range(HEADS_PER_GROUP):
        sl = slice(h * HEAD_DIM, (h + 1) * HEAD_DIM)
        q = q_ref[:, sl].astype(BF16)
        k = jnp.concatenate([kp_ref[:, sl], kc_ref[:, sl]], axis=0).astype(BF16)
        v = jnp.concatenate([vp_ref[:, sl], vc_ref[:, sl]], axis=0).astype(BF16)
        s = _dot_nt(q, k) * ATTN_SCALE + bias_ref[h]
        s = jnp.where(prev_invalid, NEG, s)
        m = jnp.max(s, axis=-1, keepdims=True)
        e = jnp.exp(s - m)
        l = jnp.sum(e, axis=-1, keepdims=True)
        acc = _dot(e.astype(BF16), v)
        o_ref[:, sl] = acc / l
        lse_ref[:, sl] = jnp.broadcast_to(m + jnp.log(l), (blk, HEAD_DIM))


def dswa_prompt(z, g, bias):
    s_len, zw = z.shape
    w, d = DSWA_GROUPS[g]
    blk = w // d
    sub = s_len // d
    nblk = sub // blk
    q_col, k_col, v_col = g, N_GROUPS + g, 2 * N_GROUPS + g
    if d == 1:
        zv, per_res = z, zw // GROUP_WIDTH
    else:
        zg = jnp.concatenate([z[:, c * GROUP_WIDTH:(c + 1) * GROUP_WIDTH] for c in (q_col, k_col, v_col)], axis=1)
        zv, per_res = zg.reshape(sub, d * 3 * GROUP_WIDTH), 3
        q_col, k_col, v_col = 0, 1, 2
    cur = lambda c: pl.BlockSpec((blk, GROUP_WIDTH), lambda r, u: (u, r * per_res + c))
    prev = lambda c: pl.BlockSpec((blk, GROUP_WIDTH), lambda r, u: (jnp.maximum(u - 1, 0), r * per_res + c))
    out_spec = pl.BlockSpec((blk, GROUP_WIDTH), lambda r, u: (u, r))
    blocks = 7 * _nbytes((blk, GROUP_WIDTH), F32) + _nbytes(bias.shape, F32)
    o, lse = pl.pallas_call(
        _dswa_prompt_kernel,
        grid=(d, nblk),
        in_specs=[cur(q_col), prev(k_col), cur(k_col), prev(v_col), cur(v_col),
                  pl.BlockSpec(bias.shape, lambda r, u: (0, 0, 0))],
        out_specs=[out_spec, out_spec],
        out_shape=[jax.ShapeDtypeStruct((sub, d * GROUP_WIDTH), F32)] * 2,
        compiler_params=_params(("parallel", "arbitrary"), blocks),
        name=f"dswa_prompt_g{g}",
    )(zv, zv, zv, zv, zv, bias)
    return o.reshape(s_len, GROUP_WIDTH), lse.reshape(s_len, GROUP_WIDTH)


def _mem_attn_prompt_kernel(q_ref, kv_ref, o_ref):
    for h in range(N_MEM_HEADS):
        sl = slice(h * HEAD_DIM, (h + 1) * HEAD_DIM)
        q = q_ref[:, sl].astype(BF16)
        k = kv_ref[:, sl].astype(BF16)
        v = kv_ref[:, MEM_WIDTH + h * HEAD_DIM:MEM_WIDTH + (h + 1) * HEAD_DIM].astype(BF16)
        s = _dot_nt(q, k) * ATTN_SCALE
        m = jnp.max(s, axis=-1, keepdims=True)
        e = jnp.exp(s - m)
        l = jnp.sum(e, axis=-1, keepdims=True)
        o_ref[:, sl] = _dot(e.astype(BF16), v) / l


def mem_attn_prompt(z, col_block, mem_kv, *, tm=512):
    n = z.shape[0]
    tm = _tile(n, tm)
    blocks = 2 * _nbytes((tm, MEM_WIDTH), F32) + _nbytes(mem_kv.shape, F32)
    return pl.pallas_call(
        _mem_attn_prompt_kernel,
        grid=(n // tm,),
        in_specs=[pl.BlockSpec((tm, MEM_WIDTH), lambda i: (i, col_block)),
                  pl.BlockSpec(mem_kv.shape, lambda i: (0, 0))],
        out_specs=pl.BlockSpec((tm, MEM_WIDTH), lambda i: (i, 0)),
        out_shape=jax.ShapeDtypeStruct((n, MEM_WIDTH), F32),
        compiler_params=_params(("parallel",), blocks, _nbytes((tm, mem_kv.shape[0]), F32) * 4),
        name="mem_attn_prompt",
    )(z, mem_kv)


ONES_ROWS = 16


def _head_keys_kernel(c_ref, w_ref, kr_ref, o_ref):
    o_ref[0, :, :QK_NOPE_DIM] = _dot(c_ref[...], w_ref[...]).astype(o_ref.dtype)
    o_ref[0, :, QK_NOPE_DIM:] = kr_ref[...]


def head_keys(c, w_uk, kr, *, tm=1024):
    n, k = c.shape
    tm = _tile(n, tm)
    blocks = _nbytes((tm, k), BF16) + _nbytes((k, QK_NOPE_DIM), BF16) + 3 * _nbytes((tm, LANES), BF16)
    return pl.pallas_call(
        _head_keys_kernel,
        grid=(N_B_HEADS, n // tm),
        in_specs=[pl.BlockSpec((tm, k), lambda h, i: (i, 0)),
                  pl.BlockSpec((k, QK_NOPE_DIM), lambda h, i: (0, h)),
                  pl.BlockSpec((tm, ROPE_PAD), lambda h, i: (i, 0))],
        out_specs=pl.BlockSpec((1, tm, Q_HEAD_PAD), lambda h, i: (h, i, 0)),
        out_shape=jax.ShapeDtypeStruct((N_B_HEADS, n, Q_HEAD_PAD), BF16),
        compiler_params=_params(("parallel", "parallel"), blocks),
        name="head_keys",
    )(c, w_uk, kr)


def _head_values_t_kernel(w_ref, c_ref, o_ref):
    o_ref[0, :V_HEAD_DIM, :] = _dot_nt(w_ref[...], c_ref[...]).astype(o_ref.dtype)
    o_ref[0, V_HEAD_DIM:, :] = jnp.ones((ONES_ROWS, o_ref.shape[2]), o_ref.dtype)


def head_values_t(w_uv_t, c, *, tn=1024):
    n, k = c.shape
    tn = _tile(n, tn)
    rows = V_HEAD_DIM + ONES_ROWS
    blocks = _nbytes((V_HEAD_DIM, k), BF16) + _nbytes((tn, k), BF16) + _nbytes((rows, tn), BF16)
    return pl.pallas_call(
        _head_values_t_kernel,
        grid=(N_B_HEADS, n // tn),
        in_specs=[pl.BlockSpec((V_HEAD_DIM, k), lambda h, j: (h, 0)), pl.BlockSpec((tn, k), lambda h, j: (j, 0))],
        out_specs=pl.BlockSpec((1, rows, tn), lambda h, j: (h, 0, j)),
        out_shape=jax.ShapeDtypeStruct((N_B_HEADS, rows, n), BF16),
        compiler_params=_params(("parallel", "parallel"), blocks),
        name="head_values_t",
    )(w_uv_t, c)


def _shift_copies(buf_ref, new_ref, nbuf_ref, sem, b):
    l_buf, t_new = buf_ref.shape[1], new_ref.shape[1]
    return (pltpu.make_async_copy(buf_ref.at[b, pl.ds(t_new, l_buf - t_new)],
                                  nbuf_ref.at[b, pl.ds(0, l_buf - t_new)], sem),
            pltpu.make_async_copy(new_ref.at[b], nbuf_ref.at[b, pl.ds(l_buf - t_new, t_new)], sem))


def _mla_prompt_kernel(q_ref, kcat_ref, vt1_ref, *rest, tq, hps, n_shift):
    shift_in, o_ref, shift_out = rest[:2 * n_shift], rest[2 * n_shift], rest[2 * n_shift + 1:3 * n_shift + 1]
    qi = pl.program_id(1)
    nv = V_HEAD_DIM

    def for_all_shift_copies(fn):
        sem = rest[-1]
        for k in range(n_shift):
            def body(b, carry, k=k):
                for cp in _shift_copies(shift_in[2 * k], shift_in[2 * k + 1], shift_out[k], sem.at[k], b):
                    fn(cp)
                return carry
            lax.fori_loop(0, shift_in[2 * k].shape[0], body, 0)

    if n_shift:
        @pl.when(jnp.logical_and(pl.program_id(0) == 0, qi == 0))
        def _():
            for_all_shift_copies(lambda cp: cp.start())

    qs = [q_ref[:, a * Q_HEAD_PAD:(a + 1) * Q_HEAD_PAD] for a in range(hps)]

    def step(kb, carry, masked):
        start = pl.multiple_of(kb * tq, tq)
        scores = [_dot_nt(kcat_ref[a, pl.ds(start, tq), :], qs[a]) for a in range(hps)]
        out = []
        for a in range(hps):
            m, acc = carry[a]
            s = scores[a]
            if masked:
                key = lax.broadcasted_iota(jnp.int32, (tq, tq), 0)
                qry = lax.broadcasted_iota(jnp.int32, (tq, tq), 1)
                s = jnp.where(key <= qry, s, NEG)
            m_new = jnp.maximum(m, jnp.max(s, axis=0, keepdims=True))
            alpha = jnp.exp(m - m_new)
            p = jnp.exp(s - m_new).astype(BF16)
            acc = alpha * acc + _dot(vt1_ref[a, :, pl.ds(start, tq)], p)
            out.append((m_new, acc))
        return tuple(out)

    init = tuple((jnp.full((1, tq), NEG, F32), jnp.zeros((vt1_ref.shape[1], tq), F32)) for _ in range(hps))
    carry = lax.fori_loop(0, qi, lambda kb, c: step(kb, c, False), init)
    carry = step(qi, carry, True)
    for a, (_, acc) in enumerate(carry):
        o_ref[:, a * nv:(a + 1) * nv] = (acc[:nv] / acc[nv:nv + 1]).T.astype(o_ref.dtype)

    if n_shift:
        @pl.when(jnp.logical_and(pl.program_id(0) == pl.num_programs(0) - 1, qi == pl.num_programs(1) - 1))
        def _():
            for_all_shift_copies(lambda cp: cp.wait())


def mla_prompt(q, kcat, vt1, shifts=(), *, tq=512, heads_per_step=4):
    s_len = q.shape[0]
    tq = _tile(s_len, tq)
    hps = heads_per_step
    vrows = vt1.shape[1]
    n_shift = len(shifts)
    blocks = _nbytes((tq, hps * Q_HEAD_PAD), BF16) + _nbytes((tq, hps * V_HEAD_DIM), BF16)
    resident = (hps * _nbytes((s_len, Q_HEAD_PAD), BF16) + hps * _nbytes((vrows, s_len), BF16)) // 2
    hbm = pl.BlockSpec(memory_space=pl.ANY)
    outs = pl.pallas_call(
        functools.partial(_mla_prompt_kernel, tq=tq, hps=hps, n_shift=n_shift),
        grid=(N_B_HEADS // hps, s_len // tq),
        in_specs=[pl.BlockSpec((tq, hps * Q_HEAD_PAD), lambda h, i: (i, h)),
                  pl.BlockSpec((hps, s_len, Q_HEAD_PAD), lambda h, i: (h, 0, 0), pipeline_mode=pl.Buffered(1)),
                  pl.BlockSpec((hps, vrows, s_len), lambda h, i: (h, 0, 0), pipeline_mode=pl.Buffered(1))]
                 + [hbm] * (2 * n_shift),
        out_specs=[pl.BlockSpec((tq, hps * V_HEAD_DIM), lambda h, i: (i, h))] + [hbm] * n_shift,
        out_shape=[jax.ShapeDtypeStruct((s_len, N_B_HEADS * V_HEAD_DIM), BF16)]
                  + [jax.ShapeDtypeStruct(buf.shape, buf.dtype) for buf, _ in shifts],
        scratch_shapes=[pltpu.SemaphoreType.DMA((n_shift,))] if n_shift else [],
        compiler_params=_params(("arbitrary", "arbitrary"), blocks + 4 * hps * _nbytes((tq, tq), F32), resident),
        name="mla_prompt",
    )(q, kcat, vt1, *[a for pair in shifts for a in pair])
    return outs[0], list(outs[1:])


def _row_scores(q, k, bias):
    return jnp.sum(k * q[None], axis=-1, keepdims=True) * ATTN_SCALE + bias


def _dswa_sample_kernel(buf_ref, new_ref, q_ref, bias_a_ref, bias_b_ref, o_ref, lse_ref):
    nh = HEADS_PER_GROUP
    t_new = new_ref.shape[1]
    k_new = new_ref[0, :, 0:nh, :]
    v_new = new_ref[0, :, nh:2 * nh, :]
    for i in range(t_new):
        q = q_ref[0, i]
        k_old = buf_ref[0, :, i, 0:nh, :]
        v_old = buf_ref[0, :, i, nh:2 * nh, :]
        s_a = _row_scores(q, k_old, bias_a_ref[i])
        s_b = _row_scores(q, k_new, bias_b_ref[i])
        m = jnp.maximum(jnp.max(s_a, axis=0), jnp.max(s_b, axis=0))
        e_a = jnp.exp(s_a - m[None])
        e_b = jnp.exp(s_b - m[None])
        l = jnp.sum(e_a, axis=0) + jnp.sum(e_b, axis=0)
        acc = jnp.sum(e_a * v_old, axis=0) + jnp.sum(e_b * v_new, axis=0)
        o_ref[0, i] = acc / l
        lse_ref[0, i] = m + jnp.log(l)


def dswa_sample(buf, new_kv, q, bias_a, bias_b, d):
    b, l_buf = buf.shape[:2]
    t_new = new_kv.shape[1]
    assert d % t_new == 0
    nsub = l_buf // d
    rows = 2 * HEADS_PER_GROUP
    bufv = buf.reshape(b, nsub, d, rows, HEAD_DIM)
    tok = lambda r: pl.BlockSpec((1, t_new, r, HEAD_DIM), lambda i: (i, 0, 0, 0))
    full = lambda a: pl.BlockSpec(a.shape, lambda i: (0,) * a.ndim)
    blocks = _nbytes((nsub, t_new, rows, HEAD_DIM), F32) + _nbytes(bias_a.shape, F32)
    return pl.pallas_call(
        _dswa_sample_kernel,
        grid=(b,),
        in_specs=[pl.BlockSpec((1, nsub, t_new, rows, HEAD_DIM), lambda i: (i, 0, 0, 0, 0)),
                  tok(rows), tok(HEADS_PER_GROUP), full(bias_a), full(bias_b)],
        out_specs=[tok(HEADS_PER_GROUP), tok(HEADS_PER_GROUP)],
        out_shape=[jax.ShapeDtypeStruct((b, t_new, HEADS_PER_GROUP, HEAD_DIM), F32)] * 2,
        compiler_params=_params(("parallel",), blocks, 4 * _nbytes((nsub, t_new, rows, HEAD_DIM), F32)),
        name=f"dswa_sample_d{d}",
    )(bufv, new_kv, q, bias_a, bias_b)


def _dswa_sample_dense_kernel(buf_ref, new_ref, q_ref, bias_ref, o_ref, lse_ref):
    nh = HEADS_PER_GROUP
    l_buf, rows = buf_ref.shape[1], buf_ref.shape[2]
    t_new = new_ref.shape[1]
    ncol = (l_buf + t_new) * rows
    data, scores = [], []
    for j in range(q_ref.shape[0]):
        src = jnp.concatenate([buf_ref[j], new_ref[j]], axis=0)
        swapped = pltpu.roll(src, nh, axis=1).reshape(ncol, HEAD_DIM).astype(BF16)
        data.append(src.reshape(ncol, HEAD_DIM).astype(BF16))
        scores.append(_dot_nt(q_ref[j].astype(BF16), swapped) * ATTN_SCALE + bias_ref[...])
    for j, s in enumerate(scores):
        m = jnp.max(s, axis=-1, keepdims=True)
        e = jnp.exp(s - m)
        l = jnp.sum(e, axis=-1, keepdims=True)
        o_ref[j] = _dot(e.astype(BF16), data[j]) / l
        lse_ref[j] = jnp.broadcast_to(m + jnp.log(l), o_ref.shape[1:])


def dswa_sample_dense(buf, new_kv, q, bias):
    b, l_buf, rows, _ = buf.shape
    seq_bytes = _nbytes((l_buf, rows, HEAD_DIM), F32)
    nb = max(1, min(8, (2 << 20) // seq_bytes))
    nb = nb if b % nb == 0 else 1
    lead = lambda a: pl.BlockSpec((nb,) + a.shape[1:], lambda i: (i,) + (0,) * (a.ndim - 1))
    blocks = nb * seq_bytes + _nbytes(bias.shape, F32)
    return pl.pallas_call(
        _dswa_sample_dense_kernel,
        grid=(b // nb,),
        in_specs=[lead(buf), lead(new_kv), lead(q), pl.BlockSpec(bias.shape, lambda i: (0, 0))],
        out_specs=[lead(q), lead(q)],
        out_shape=[jax.ShapeDtypeStruct(q.shape, F32)] * 2,
        compiler_params=_params(("parallel",), blocks, 2 * nb * seq_bytes),
        name=f"dswa_sample_dense_l{l_buf}",
    )(buf, new_kv, q, bias)


def _mem_attn_sample_kernel(kv_ref, q_ref, o_ref):
    nh = N_MEM_HEADS
    nq = q_ref.shape[1]
    n_mem, rows = kv_ref.shape[2], kv_ref.shape[3]
    ncol = n_mem * rows
    col_row = lax.broadcasted_iota(jnp.int32, (nq, ncol), 1) % rows
    q_head = lax.broadcasted_iota(jnp.int32, (nq, ncol), 0) // (nq // nh)
    own = col_row == q_head + nh
    data, scores = [], []
    for j in range(q_ref.shape[0]):
        x = kv_ref[0, j]
        swapped = pltpu.roll(x, nh, axis=1).reshape(ncol, HEAD_DIM).astype(BF16)
        data.append(x.reshape(ncol, HEAD_DIM).astype(BF16))
        scores.append(_dot_nt(q_ref[j].astype(BF16), swapped) * ATTN_SCALE)
    for j, s in enumerate(scores):
        s = jnp.where(own, s, NEG)
        e = jnp.exp(s - jnp.max(s, axis=-1, keepdims=True))
        o_ref[j] = _dot(e.astype(BF16), data[j]) / jnp.sum(e, axis=-1, keepdims=True)


def mem_attn_sample(cache, layer, q, *, seqs_per_step=4):
    _, b, n_mem, rows, _ = cache.shape
    nb = seqs_per_step if b % seqs_per_step == 0 else 1
    tok = pl.BlockSpec((nb,) + q.shape[1:], lambda i: (i, 0, 0))
    return pl.pallas_call(
        _mem_attn_sample_kernel,
        grid=(b // nb,),
        in_specs=[pl.BlockSpec((1, nb, n_mem, rows, HEAD_DIM), lambda i: (layer, i, 0, 0, 0)), tok],
        out_specs=tok,
        out_shape=jax.ShapeDtypeStruct(q.shape, F32),
        compiler_params=_params(("parallel",), nb * _nbytes((n_mem, rows, HEAD_DIM), F32),
                                2 * nb * _nbytes((n_mem, rows, HEAD_DIM), F32)),
        name="mem_attn_sample",
    )(cache, q)


def _mla_sample_kernel(pt_ref, q_ref, new_ref, *rest, n_pages, n_seq, t_new):
    del pt_ref
    page_refs = rest[:n_pages * n_seq]
    o_ref, m_ref, l_ref, acc_ref = rest[n_pages * n_seq:]
    c = pl.program_id(1)
    nq = q_ref.shape[1]

    @pl.when(c == 0)
    def _():
        m_ref[...] = jnp.full(m_ref.shape, NEG, F32)
        l_ref[...] = jnp.zeros(l_ref.shape, F32)
        acc_ref[...] = jnp.zeros(acc_ref.shape, F32)

    def update(j, s_parts, pv_fn):
        m_prev = m_ref[j]
        m_cur = jnp.max(functools.reduce(jnp.maximum, s_parts), axis=-1, keepdims=True)
        m_new = jnp.maximum(m_prev, m_cur)
        alpha = jnp.exp(m_prev - m_new)
        p = [jnp.exp(s - m_new) for s in s_parts]
        l_ref[j] = alpha * l_ref[j] + jnp.sum(functools.reduce(jnp.add, p), axis=-1, keepdims=True)
        m_ref[j] = m_new
        acc_ref[j] = alpha * acc_ref[j] + pv_fn([x.astype(BF16) for x in p])

    pages = [[r[0].astype(BF16) for r in page_refs[j * n_pages:(j + 1) * n_pages]] for j in range(n_seq)]
    scores = [[_dot(q_ref[j], kt) for kt in pages[j]] for j in range(n_seq)]
    for j in range(n_seq):
        update(j, scores[j], lambda p, kts=pages[j]: functools.reduce(
            jnp.add, [_dot_nt(pi, kt[:KV_LORA_RANK]) for pi, kt in zip(p, kts)]))

    @pl.when(c == pl.num_programs(1) - 1)
    def _():
        tok = lax.broadcasted_iota(jnp.int32, (nq, t_new), 0) // (nq // t_new)
        key = lax.broadcasted_iota(jnp.int32, (nq, t_new), 1)
        for j in range(n_seq):
            nr = new_ref[j].astype(BF16)
            s = jnp.where(key <= tok, _dot_nt(q_ref[j], nr), NEG)
            update(j, [s], lambda p, nr=nr: _dot(p[0], nr[:, :KV_LORA_RANK]))
            o_ref[j] = acc_ref[j] / l_ref[j]


def mla_sample(q, new_rows, cache_t, page_table):
    b, nq, row = q.shape
    t_new = new_rows.shape[1]
    n_seq_pages = page_table.shape[1]
    pps = PAGES_PER_STEP if n_seq_pages % PAGES_PER_STEP == 0 else 1
    nb = SEQS_PER_STEP if b % SEQS_PER_STEP == 0 else 1

    def page_spec(j, p):
        return pl.BlockSpec((1, row, PAGE_SIZE), lambda i, c, pt: (pt[i * nb + j, c * pps + p], 0, 0))

    blocks = (nb * pps * _nbytes((row, PAGE_SIZE), F32) + nb * _nbytes((nq, row), BF16)
              + nb * _nbytes((nq, KV_LORA_RANK), F32))
    return pl.pallas_call(
        functools.partial(_mla_sample_kernel, n_pages=pps, n_seq=nb, t_new=t_new),
        grid_spec=pltpu.PrefetchScalarGridSpec(
            num_scalar_prefetch=1,
            grid=(b // nb, n_seq_pages // pps),
            in_specs=[pl.BlockSpec((nb, nq, row), lambda i, c, pt: (i, 0, 0)),
                      pl.BlockSpec((nb, t_new, row), lambda i, c, pt: (i, 0, 0))]
                     + [page_spec(j, p) for j in range(nb) for p in range(pps)],
            out_specs=pl.BlockSpec((nb, nq, KV_LORA_RANK), lambda i, c, pt: (i, 0, 0)),
            scratch_shapes=[pltpu.VMEM((nb, nq, 1), F32), pltpu.VMEM((nb, nq, 1), F32),
                            pltpu.VMEM((nb, nq, KV_LORA_RANK), F32)]),
        out_shape=jax.ShapeDtypeStruct((b, nq, KV_LORA_RANK), F32),
        compiler_params=_params(("parallel", "arbitrary"), blocks,
                                nb * pps * _nbytes((row, PAGE_SIZE), BF16)),
        name="mla_sample",
    )(page_table, q, new_rows, *([cache_t] * (nb * pps)))


def _t5_bucket(dist):
    max_exact = N_BUCKETS // 2
    dd = jnp.maximum(dist, 1).astype(F32)
    large = max_exact + (jnp.log(dd / max_exact) / math.log(MAX_DISTANCE / max_exact)
                         * (N_BUCKETS - max_exact)).astype(jnp.int32)
    large = jnp.minimum(large, N_BUCKETS - 1)
    return jnp.where(dist < max_exact, dist, large)


def _group_bias(t5_bias, g):
    w, d = DSWA_GROUPS[g]
    offs = jnp.arange(w // d + 1, dtype=jnp.int32) * d
    return t5_bias[_t5_bucket(offs), g * HEADS_PER_GROUP:(g + 1) * HEADS_PER_GROUP].astype(F32)


def _banded(table, j):
    jmax = table.shape[0] - 1
    vals = table[jnp.clip(j, 0, jmax)]
    return jnp.where(((j >= 0) & (j <= jmax))[..., None], vals, NEG)


def _prompt_bias(table):
    blk, nh = table.shape[0] - 1, table.shape[1]
    n = 3 * blk - 1
    pad = jnp.full((blk - 1, nh), NEG, F32)
    u = jnp.concatenate([pad, table, pad], axis=0)
    skew = jnp.tile(u, (blk + 1, 1))[:blk * (n + 1)].reshape(blk, n + 1, nh)
    return jnp.moveaxis(skew[:, :2 * blk][:, ::-1], -1, 0)


def _sample_bias(table, d, t_new):
    blk = table.shape[0] - 1
    i = jnp.arange(t_new)
    j_a = blk - jnp.arange(blk)[None, :] + (i // d)[:, None]
    diff = i[:, None] - i[None, :]
    j_b = jnp.where((diff >= 0) & (diff % d == 0), diff // d, -1)
    lanes = lambda x: jnp.broadcast_to(x[..., None], x.shape + (HEAD_DIM,))
    return lanes(_banded(table, j_a)), lanes(_banded(table, j_b))


def _sample_bias_dense(table, d, t_new):
    blk, nh = table.shape[0] - 1, table.shape[1]
    w = blk * d
    stuffed = jnp.concatenate([table[:, None, :], jnp.full((blk + 1, d - 1, nh), NEG, F32)], axis=1)
    stuffed = stuffed.reshape((blk + 1) * d, nh)[:w + 1]
    pad = jnp.full((t_new - 1, nh), NEG, F32)
    rev = jnp.concatenate([pad, stuffed, pad], axis=0)[::-1]
    band = jnp.stack([rev[t_new - 1 - i:t_new - 1 - i + w + t_new] for i in range(t_new)])
    band = jnp.transpose(band, (2, 0, 1))[..., None]
    head = jnp.arange(nh)[:, None, None, None]
    sub = jnp.arange(2 * nh)[None, None, None, :]
    return jnp.where(sub == head + nh, band, NEG).reshape(nh * t_new, (w + t_new) * 2 * nh)


def _rope_tables(pos):
    half = QK_ROPE_DIM // 2
    inv_freq = ROPE_THETA ** (-jnp.arange(half, dtype=F32) / half)
    ang = pos.astype(F32)[:, None] * inv_freq[None, :]
    cos, sin = jnp.cos(ang), jnp.sin(ang)
    zero = jnp.zeros_like(cos)
    cc = jnp.concatenate([cos, cos, zero, zero], axis=-1)
    shi = jnp.concatenate([zero, sin, zero, zero], axis=-1)
    slo = jnp.concatenate([-sin, zero, zero, zero], axis=-1)
    return cc, shi, slo


def _pad_last(w, to):
    return jnp.pad(w, [(0, 0)] * (w.ndim - 1) + [(0, to - w.shape[-1])])


def _trunk(x, rope_tables, dswa_fn, mem_fn, mla_fn, p, z0=None):
    if z0 is None:
        z0 = norm_matmul(x, p['g_attn_pre'][0], p['w_a_in'])
    o_tok = dswa_fn(z0)
    o_mem = mem_fn(0, z0, A_QKV_WIDTH // MEM_WIDTH)
    x = out_proj(o_tok, o_mem, p['w_a_out'][:GROUP_WIDTH], p['w_a_out'][GROUP_WIDTH:], p['g_attn_post'][0], x)
    x = mlp(x, p['g_mlp_pre'][0], p['w_mlp_up'][0], p['w_mlp_down'][0], p['g_mlp_post'][0])
    raw = norm_matmul(x, p['g_kv_in'], p['w_kv_down'])
    rows, c_bf, kr_bf = kv_finish(raw, p['g_kv_latent'], rope_tables)
    z = norm_matmul(x, p['g_attn_pre'][1], p['w_b_in'])
    q = norm_matmul(z, p['g_q_latent'], p['w_q_up'], k_width=Q_LORA_RANK, rope_tables=rope_tables,
                    scale=MLA_SCALE, out_dtype=BF16, tn=2 * Q_HEAD_PAD)
    o_tok = mla_fn(q, rows, c_bf, kr_bf)
    o_mem = mem_fn(1, z, Q_LORA_RANK // MEM_WIDTH)
    nv = N_B_HEADS * V_HEAD_DIM
    x = out_proj(o_tok, o_mem, p['w_b_out'][:nv], p['w_b_out'][nv:], p['g_attn_post'][1], x)
    x = mlp(x, p['g_mlp_pre'][1], p['w_mlp_up'][1], p['w_mlp_down'][1], p['g_mlp_post'][1])
    return x, rows[:, :MLA_ROW], z0


def kernel(x_prompt, x_sample, mem_prompt, cache_swa_kv_w128, cache_swa_kv_w512, cache_swa_kv_w2048, cache_mla_kv, cache_mem_kv, page_table, t5_bias, g_attn_pre, g_attn_post, g_mlp_pre, g_mlp_post, g_mem, w_mem_kv, w_mlp_up, w_mlp_down, w_a_in, w_a_out, g_kv_in, w_kv_down, g_kv_latent, w_kv_up, w_b_in, g_q_latent, w_q_up, w_b_out):
    depth = g_attn_pre.shape[0]
    assert depth == 2 and w_a_in.shape[0] == 1 and w_b_in.shape[0] == 1
    bp, s_len, d_model = x_prompt.shape
    assert bp == 1
    bs, t_new, _ = x_sample.shape
    past = page_table.shape[1] * PAGE_SIZE
    bufs = (cache_swa_kv_w128, cache_swa_kv_w512, cache_swa_kv_w2048)
    for buf, (w, d) in zip(bufs, DSWA_GROUPS):
        assert buf.shape[2] == w and past >= w and s_len % (d * (w // d)) == 0 and s_len >= w

    w_q_up_pad = _pad_last(w_q_up[0].reshape(Q_LORA_RANK, N_B_HEADS, QK_NOPE_DIM + QK_ROPE_DIM),
                           Q_HEAD_PAD).reshape(Q_LORA_RANK, N_B_HEADS * Q_HEAD_PAD)
    p = {
        'g_attn_pre': g_attn_pre, 'g_attn_post': g_attn_post, 'g_mlp_pre': g_mlp_pre, 'g_mlp_post': g_mlp_post,
        'g_kv_in': g_kv_in, 'g_kv_latent': g_kv_latent, 'g_q_latent': g_q_latent[0],
        'w_a_in': w_a_in[0].astype(BF16), 'w_a_out': w_a_out[0].astype(BF16),
        'w_b_in': w_b_in[0].astype(BF16), 'w_b_out': w_b_out[0].astype(BF16),
        'w_mlp_up': w_mlp_up.astype(BF16), 'w_mlp_down': w_mlp_down.astype(BF16),
        'w_kv_down': _pad_last(w_kv_down, MLA_ROW_PAD).astype(BF16),
        'w_q_up': w_q_up_pad.astype(BF16),
    }
    w_uk = w_kv_up[..., :QK_NOPE_DIM]
    w_uv = w_kv_up[..., QK_NOPE_DIM:]
    w_uk_cat = w_uk.reshape(KV_LORA_RANK, -1).astype(BF16)
    w_uv_cat_t = w_uv.reshape(KV_LORA_RANK, -1).T.astype(BF16)
    w_uk_t =jnp.transpose(w_uk, (1, 2, 0)).astype(BF16)
    w_uv_h = jnp.transpose(w_uv, (1, 0, 2)).astype(BF16)
    tables = [_group_bias(t5_bias, g) for g in range(N_GROUPS)]

    ns = bs * t_new

    def heads(a, nh):
        return a.reshape(bs, t_new, nh, HEAD_DIM)

    z0_s = norm_matmul(x_sample.reshape(ns, d_model), g_attn_pre[0], p['w_a_in'])
    section = lambda c, g: z0_s[:, (c * N_GROUPS + g) * GROUP_WIDTH:(c * N_GROUPS + g + 1) * GROUP_WIDTH]
    new_kvs = [jnp.concatenate([heads(section(1, g), HEADS_PER_GROUP), heads(section(2, g), HEADS_PER_GROUP)], axis=2)
               for g in range(N_GROUPS)]
    bufs4 = [buf[0].reshape(bs, w, 2 * HEADS_PER_GROUP, HEAD_DIM) for buf, (w, _) in zip(bufs, DSWA_GROUPS)]
    swa_s = []

    mem_kv_prompt = [norm_matmul(mem_prompt[0], g_mem[l], w_mem_kv[l].astype(BF16)) for l in range(depth)]

    def dswa_p(z):
        parts = [dswa_prompt(z, g, _prompt_bias(tables[g])) for g in range(N_GROUPS)]
        return combine_groups([o for o, _ in parts], [l for _, l in parts])

    def mla_p(q, rows, c_bf, kr_bf):
        o, new_bufs = mla_prompt(q, head_keys(c_bf, w_uk_cat, kr_bf), head_values_t(w_uv_cat_t, c_bf),
                                 list(zip(bufs4, new_kvs)))
        swa_s.extend(nb.reshape(buf.shape) for nb, buf in zip(new_bufs, bufs))
        return o

    y_p, rows_p, z_p = _trunk(x_prompt[0], _rope_tables(jnp.arange(s_len, dtype=jnp.int32)), dswa_p,
                              lambda l, z, cb: mem_attn_prompt(z, cb, mem_kv_prompt[l]), mla_p, p)
    swa_p = []
    for g, (w, _) in enumerate(DSWA_GROUPS):
        k_g = z_p[s_len - w:, (N_GROUPS + g) * GROUP_WIDTH:(N_GROUPS + g + 1) * GROUP_WIDTH]
        v_g = z_p[s_len - w:, (2 * N_GROUPS + g) * GROUP_WIDTH:(2 * N_GROUPS + g + 1) * GROUP_WIDTH]
        kv_g = jnp.stack([k_g.reshape(w, HEADS_PER_GROUP, HEAD_DIM), v_g.reshape(w, HEADS_PER_GROUP, HEAD_DIM)], axis=1)
        swa_p.append(kv_g[None, None])
    mem_kv_out = jnp.stack(mem_kv_prompt).reshape(depth, 1, mem_prompt.shape[1], 2, N_MEM_HEADS, HEAD_DIM)

    pos_s = past + (jnp.arange(ns, dtype=jnp.int32) % t_new)

    def dswa_s(z):
        outs, lses = [], []
        for g, (w, d) in enumerate(DSWA_GROUPS):
            q = heads(section(0, g), HEADS_PER_GROUP)
            if w <= DENSE_SAMPLE_MAX_ROWS:
                q = jnp.swapaxes(q, 1, 2).reshape(bs, HEADS_PER_GROUP * t_new, HEAD_DIM)
                o, lse = dswa_sample_dense(bufs4[g], new_kvs[g], q, _sample_bias_dense(tables[g], d, t_new))
                o, lse = (jnp.swapaxes(a.reshape(bs, HEADS_PER_GROUP, t_new, HEAD_DIM), 1, 2) for a in (o, lse))
            else:
                bias_a, bias_b = _sample_bias(tables[g], d, t_new)
                o, lse = dswa_sample(bufs4[g], new_kvs[g], q, bias_a, bias_b, d)
            outs.append(o.reshape(ns, GROUP_WIDTH))
            lses.append(lse.reshape(ns, GROUP_WIDTH))
        return combine_groups(outs, lses)

    cache_mem = cache_mem_kv.reshape(depth, bs, cache_mem_kv.shape[2], 2 * N_MEM_HEADS, HEAD_DIM)

    def mem_s(l, z, cb):
        q = heads(z[:, cb * MEM_WIDTH:(cb + 1) * MEM_WIDTH], N_MEM_HEADS)
        q = jnp.swapaxes(q, 1, 2).reshape(bs, N_MEM_HEADS * t_new, HEAD_DIM)
        o = mem_attn_sample(cache_mem, l, q).reshape(bs, N_MEM_HEADS, t_new, HEAD_DIM)
        return jnp.swapaxes(o, 1, 2).reshape(ns, MEM_WIDTH)

    cache_mla_t = jnp.swapaxes(cache_mla_kv, 1, 2)

    def mla_s(q, rows, c_bf, kr_bf):
        q_lat = head_matmul(q, w_uk_t, x_block_stride=Q_HEAD_PAD // QK_NOPE_DIM, out_dtype=BF16)
        q3 = q.reshape(ns, N_B_HEADS, Q_HEAD_PAD)
        q_full = jnp.concatenate([q_lat.reshape(ns, N_B_HEADS, KV_LORA_RANK),
                                  q3[:, :, QK_NOPE_DIM:QK_NOPE_DIM + QK_ROPE_DIM]], axis=-1)
        q_full = q_full.reshape(bs, t_new * N_B_HEADS, MLA_ROW)
        new_rows = rows[:, :MLA_ROW].reshape(bs, t_new, MLA_ROW)
        o_lat = mla_sample(q_full, new_rows, cache_mla_t, page_table)
        return head_matmul(o_lat.reshape(ns, N_B_HEADS * KV_LORA_RANK), w_uv_h, out_dtype=BF16)

    y_s, rows_s, _ = _trunk(x_sample.reshape(ns, d_model), _rope_tables(pos_s), dswa_s, mem_s, mla_s, p, z0=z0_s)

    return (y_p[None], y_s.reshape(bs, t_new, d_model), swa_p[0], swa_p[1], swa_p[2],
            rows_p[None], mem_kv_out, swa_s[0], swa_s[1], swa_s[2], rows_s.reshape(bs, t_new, MLA_ROW))
```

```python
import functools
import math

import jax
import jax.numpy as jnp
from jax import lax
from jax.experimental import pallas as pl
from jax.experimental.pallas import tpu as pltpu

F32 = jnp.float32
BF16 = jnp.bfloat16

HEAD_DIM = 128
DSWA_GROUPS = ((128, 1), (512, 4), (2048, 16))
N_GROUPS = len(DSWA_GROUPS)
HEADS_PER_GROUP = 4
GROUP_WIDTH = HEADS_PER_GROUP * HEAD_DIM
N_A_HEADS = N_GROUPS * HEADS_PER_GROUP
A_QKV_WIDTH = 3 * N_A_HEADS * HEAD_DIM
N_MEM_HEADS = 4
MEM_WIDTH = N_MEM_HEADS * HEAD_DIM
N_B_HEADS = 12
Q_LORA_RANK = 1536
KV_LORA_RANK = 512
QK_NOPE_DIM = 128
QK_ROPE_DIM = 64
V_HEAD_DIM = 128
MLA_ROW = KV_LORA_RANK + QK_ROPE_DIM
MLA_SCALE = (QK_NOPE_DIM + QK_ROPE_DIM) ** -0.5
ATTN_SCALE = HEAD_DIM ** -0.5
ROPE_THETA = 10000.0
N_BUCKETS = 32
MAX_DISTANCE = 2048
EPS = 1e-6
PAGE_SIZE = 128

LANES = 128
ROPE_PAD = LANES
Q_HEAD_PAD = QK_NOPE_DIM + ROPE_PAD
MLA_ROW_PAD = KV_LORA_RANK + ROPE_PAD
NEG = -1e30
VMEM_BYTES_V7X = 64 * 1024 * 1024
VMEM_CAP = VMEM_BYTES_V7X - 8 * 1024 * 1024
PAGES_PER_STEP = 8
SEQS_PER_STEP = 4
DENSE_SAMPLE_MAX_ROWS = 512


def _tile(n, pref):
    return pref if n % pref == 0 else n


def _nbytes(shape, dtype):
    return math.prod(shape) * jnp.dtype(dtype).itemsize


def _params(semantics, pipelined_bytes, resident_bytes=0):
    est = 2 * pipelined_bytes + 2 * resident_bytes + (4 << 20)
    return pltpu.CompilerParams(dimension_semantics=semantics,
                                vmem_limit_bytes=int(min(max(est, 16 << 20), VMEM_CAP)))


def _rms(x, g):
    return x * lax.rsqrt(jnp.mean(x * x, axis=-1, keepdims=True) + EPS) * g


def _rope_pad(r, cc, shi, slo):
    half = QK_ROPE_DIM // 2
    return r * cc + pltpu.roll(r, half, axis=1) * shi + pltpu.roll(r, LANES - half, axis=1) * slo


def _dot(a, b):
    return jnp.dot(a, b, preferred_element_type=F32)


def _dot_nt(a, b):
    return lax.dot_general(a, b, (((1,), (1,)), ((), ())), preferred_element_type=F32)


def _norm_matmul_kernel(*refs, norm, rope, scale):
    if rope:
        x_ref, g_ref, w_ref, cc_ref, shi_ref, slo_ref, o_ref, xn_ref = refs
    else:
        x_ref, g_ref, w_ref, o_ref, xn_ref = refs

    @pl.when(pl.program_id(1) == 0)
    def _():
        x = x_ref[...].astype(F32)
        if norm:
            x = _rms(x, g_ref[...])
        xn_ref[...] = x.astype(BF16)

    acc = _dot(xn_ref[...], w_ref[...])
    if scale != 1.0:
        acc = acc * scale
    if rope:
        cc, shi, slo = cc_ref[...], shi_ref[...], slo_ref[...]
        for h in range(acc.shape[1] // Q_HEAD_PAD):
            lo = h * Q_HEAD_PAD
            o_ref[:, lo:lo + QK_NOPE_DIM] = acc[:, lo:lo + QK_NOPE_DIM].astype(o_ref.dtype)
            r = _rope_pad(acc[:, lo + QK_NOPE_DIM:lo + Q_HEAD_PAD], cc, shi, slo)
            o_ref[:, lo + QK_NOPE_DIM:lo + Q_HEAD_PAD] = r.astype(o_ref.dtype)
    else:
        o_ref[...] = acc.astype(o_ref.dtype)


def norm_matmul(x, g, w, *, k_width=None, norm=True, rope_tables=None, scale=1.0,
                out_dtype=F32, tm=1024, tn=512):
    n = x.shape[0]
    k = k_width or x.shape[1]
    nout = w.shape[1]
    tm = _tile(n, tm)
    tn = _tile(nout, tn)
    rope = rope_tables is not None
    in_specs = [pl.BlockSpec((tm, k), lambda i, j: (i, 0)),
                pl.BlockSpec((1, k), lambda i, j: (0, 0)),
                pl.BlockSpec((k, tn), lambda i, j: (0, j))]
    args = [x, g.reshape(1, k).astype(F32), w]
    if rope:
        in_specs += [pl.BlockSpec((tm, LANES), lambda i, j: (i, 0))] * 3
        args += list(rope_tables)
    blocks = (_nbytes((tm, k), x.dtype) + _nbytes((k, tn), w.dtype) + _nbytes((tm, tn), out_dtype)
              + 3 * _nbytes((tm, LANES), F32))
    return pl.pallas_call(
        functools.partial(_norm_matmul_kernel, norm=norm, rope=rope, scale=scale),
        grid=(n // tm, nout // tn),
        in_specs=in_specs,
        out_specs=pl.BlockSpec((tm, tn), lambda i, j: (i, j)),
        out_shape=jax.ShapeDtypeStruct((n, nout), out_dtype),
        scratch_shapes=[pltpu.VMEM((tm, k), BF16)],
        compiler_params=_params(("parallel", "arbitrary"), blocks,
                                _nbytes((tm, k), F32) + _nbytes((tm, tn), F32)),
        name="norm_matmul",
    )(*args)


def _kv_finish_kernel(raw_ref, g_ref, cc_ref, shi_ref, slo_ref, rows_ref, c_ref, kr_ref):
    c = _rms(raw_ref[:, :KV_LORA_RANK], g_ref[...])
    kr = _rope_pad(raw_ref[:, KV_LORA_RANK:], cc_ref[...], shi_ref[...], slo_ref[...])
    rows_ref[:, :KV_LORA_RANK] = c
    rows_ref[:, KV_LORA_RANK:] = kr
    c_ref[...] = c.astype(BF16)
    kr_ref[...] = kr.astype(BF16)


def kv_finish(raw, g_latent, rope_tables, *, tm=512):
    n = raw.shape[0]
    tm = _tile(n, tm)
    row = lambda w: pl.BlockSpec((tm, w), lambda i: (i, 0))
    blocks = 2 * _nbytes((tm, MLA_ROW_PAD), F32) + 4 * _nbytes((tm, LANES), F32) + _nbytes((tm, KV_LORA_RANK), BF16)
    return pl.pallas_call(
        _kv_finish_kernel,
        grid=(n // tm,),
        in_specs=[row(MLA_ROW_PAD), pl.BlockSpec((1, KV_LORA_RANK), lambda i: (0, 0)),
                  row(LANES), row(LANES), row(LANES)],
        out_specs=[row(MLA_ROW_PAD), row(KV_LORA_RANK), row(LANES)],
        out_shape=[jax.ShapeDtypeStruct((n, MLA_ROW_PAD), F32),
                   jax.ShapeDtypeStruct((n, KV_LORA_RANK), BF16),
                   jax.ShapeDtypeStruct((n, LANES), BF16)],
        compiler_params=_params(("parallel",), blocks),
        name="kv_finish",
    )(raw, g_latent.reshape(1, -1).astype(F32), *rope_tables)


def _out_proj_kernel(a1_ref, a2_ref, w1_ref, w2_ref, g_ref, x_ref, o_ref):
    o = _dot(a1_ref[...].astype(BF16), w1_ref[...]) + _dot(a2_ref[...].astype(BF16), w2_ref[...])
    o_ref[...] = x_ref[...] + _rms(o, g_ref[...])


def out_proj(a1, a2, w1, w2, g, x, *, tm=512):
    n, d = x.shape
    k1, k2 = a1.shape[1], a2.shape[1]
    tm = _tile(n, tm)
    blocks = (_nbytes((tm, k1), a1.dtype) + _nbytes((tm, k2), a2.dtype) + _nbytes((k1 + k2, d), BF16)
              + 2 * _nbytes((tm, d), F32))
    return pl.pallas_call(
        _out_proj_kernel,
        grid=(n // tm,),
        in_specs=[pl.BlockSpec((tm, k1), lambda i: (i, 0)), pl.BlockSpec((tm, k2), lambda i: (i, 0)),
                  pl.BlockSpec((k1, d), lambda i: (0, 0)), pl.BlockSpec((k2, d), lambda i: (0, 0)),
                  pl.BlockSpec((1, d), lambda i: (0, 0)), pl.BlockSpec((tm, d), lambda i: (i, 0))],
        out_specs=pl.BlockSpec((tm, d), lambda i: (i, 0)),
        out_shape=jax.ShapeDtypeStruct((n, d), F32),
        compiler_params=_params(("parallel",), blocks, _nbytes((tm, d), F32)),
        name="out_proj",
    )(a1, a2, w1, w2, g.reshape(1, d).astype(F32), x)


def _mlp_kernel(x_ref, gpre_ref, wup_ref, wdn_ref, gpost_ref, o_ref, xn_ref, acc_ref):
    j = pl.program_id(1)

    @pl.when(j == 0)
    def _():
        xn_ref[...] = _rms(x_ref[...], gpre_ref[...]).astype(BF16)

    h = _dot(xn_ref[...], wup_ref[...])
    u = jnp.square(jnp.maximum(h, 0.0)).astype(BF16)
    part = _dot(u, wdn_ref[...])

    @pl.when(j == 0)
    def _():
        acc_ref[...] = part

    @pl.when(j > 0)
    def _():
        acc_ref[...] += part

    @pl.when(j == pl.num_programs(1) - 1)
    def _():
        o_ref[...] = x_ref[...] + _rms(acc_ref[...], gpost_ref[...])


def mlp(x, g_pre, w_up, w_down, g_post, *, tm=512, tf=1024):
    n, d = x.shape
    f = w_up.shape[1]
    tm = _tile(n, tm)
    tf = _tile(f, tf)
    blocks = 2 * _nbytes((tm, d), F32) + 2 * _nbytes((d, tf), BF16)
    resident = _nbytes((tm, d), BF16) + _nbytes((tm, d), F32) + _nbytes((tm, tf), F32)
    return pl.pallas_call(
        _mlp_kernel,
        grid=(n // tm, f // tf),
        in_specs=[pl.BlockSpec((tm, d), lambda i, j: (i, 0)), pl.BlockSpec((1, d), lambda i, j: (0, 0)),
                  pl.BlockSpec((d, tf), lambda i, j: (0, j)), pl.BlockSpec((tf, d), lambda i, j: (j, 0)),
                  pl.BlockSpec((1, d), lambda i, j: (0, 0))],
        out_specs=pl.BlockSpec((tm, d), lambda i, j: (i, 0)),
        out_shape=jax.ShapeDtypeStruct((n, d), F32),
        scratch_shapes=[pltpu.VMEM((tm, d), BF16), pltpu.VMEM((tm, d), F32)],
        compiler_params=_params(("parallel", "arbitrary"), blocks, resident),
        name="mlp",
    )(x, g_pre.reshape(1, d).astype(F32), w_up, w_down, g_post.reshape(1, d).astype(F32))


def _head_matmul_kernel(x_ref, w_ref, o_ref):
    o_ref[...] = _dot(x_ref[...].astype(BF16), w_ref[0]).astype(o_ref.dtype)


def head_matmul(x, w, *, x_block_stride=1, out_dtype=F32):
    n = x.shape[0]
    nh, kh, nn = w.shape
    blocks = _nbytes((n, kh), x.dtype) + _nbytes((kh, nn), w.dtype) + _nbytes((n, nn), out_dtype)
    return pl.pallas_call(
        _head_matmul_kernel,
        grid=(nh,),
        in_specs=[pl.BlockSpec((n, kh), lambda h: (0, h * x_block_stride)),
                  pl.BlockSpec((1, kh, nn), lambda h: (h, 0, 0))],
        out_specs=pl.BlockSpec((n, nn), lambda h: (0, h)),
        out_shape=jax.ShapeDtypeStruct((n, nh * nn), out_dtype),
        compiler_params=_params(("parallel",), blocks),
        name="head_matmul",
    )(x, w)


def _combine_kernel(o0, o1, o2, l0, l1, l2, out_ref):
    m = jnp.maximum(jnp.maximum(l0[...], l1[...]), l2[...])
    w0, w1, w2 = jnp.exp(l0[...] - m), jnp.exp(l1[...] - m), jnp.exp(l2[...] - m)
    out_ref[...] = (w0 * o0[...] + w1 * o1[...] + w2 * o2[...]) / (w0 + w1 + w2)


def combine_groups(outs, lses, *, tm=512):
    n, w = outs[0].shape
    tm = _tile(n, tm)
    spec = pl.BlockSpec((tm, w), lambda i: (i, 0))
    return pl.pallas_call(
        _combine_kernel,
        grid=(n // tm,),
        in_specs=[spec] * 6,
        out_specs=spec,
        out_shape=jax.ShapeDtypeStruct((n, w), F32),
        compiler_params=_params(("parallel",), 7 * _nbytes((tm, w), F32)),
        name="combine_groups",
    )(*outs, *lses)


def _dswa_prompt_kernel(q_ref, kp_ref, kc_ref, vp_ref, vc_ref, bias_ref, o_ref, lse_ref):
    blk = q_ref.shape[0]
    col = lax.broadcasted_iota(jnp.int32, (blk, 2 * blk), 1)
    prev_invalid = col < jnp.where(pl.program_id(1) == 0, blk, 0)
    for h in range(HEADS_PER_GROUP):
        sl = slice(h * HEAD_DIM, (h + 1) * HEAD_DIM)
        q = q_ref[:, sl].astype(BF16)
        k = jnp.concatenate([kp_ref[:, sl], kc_ref[:, sl]], axis=0).astype(BF16)
        v = jnp.concatenate([vp_ref[:, sl], vc_ref[:, sl]], axis=0).astype(BF16)
        s = _dot_nt(q, k) * ATTN_SCALE + bias_ref[h]
        s = jnp.where(prev_invalid, NEG, s)
        m = jnp.max(s, axis=-1, keepdims=True)
        e = jnp.exp(s - m)
        l = jnp.sum(e, axis=-1, keepdims=True)
        acc = _dot(e.astype(BF16), v)
        o_ref[:, sl] = acc / l
        lse_ref[:, sl] = jnp.broadcast_to(m + jnp.log(l), (blk, HEAD_DIM))


def dswa_prompt(z, g, bias):
    s_len, zw = z.shape
    w, d = DSWA_GROUPS[g]
    blk = w // d
    sub = s_len // d
    nblk = sub // blk
    q_col, k_col, v_col = g, N_GROUPS + g, 2 * N_GROUPS + g
    if d == 1:
        zv, per_res = z, zw // GROUP_WIDTH
    else:
        zg = jnp.concatenate([z[:, c * GROUP_WIDTH:(c + 1) * GROUP_WIDTH] for c in (q_col, k_col, v_col)], axis=1)
        zv, per_res = zg.reshape(sub, d * 3 * GROUP_WIDTH), 3
        q_col, k_col, v_col = 0, 1, 2
    cur = lambda c: pl.BlockSpec((blk, GROUP_WIDTH), lambda r, u: (u, r * per_res + c))
    prev = lambda c: pl.BlockSpec((blk, GROUP_WIDTH), lambda r, u: (jnp.maximum(u - 1, 0), r * per_res + c))
    out_spec = pl.BlockSpec((blk, GROUP_WIDTH), lambda r, u: (u, r))
    blocks = 7 * _nbytes((blk, GROUP_WIDTH), F32) + _nbytes(bias.shape, F32)
    o, lse = pl.pallas_call(
        _dswa_prompt_kernel,
        grid=(d, nblk),
        in_specs=[cur(q_col), prev(k_col), cur(k_col), prev(v_col), cur(v_col),
                  pl.BlockSpec(bias.shape, lambda r, u: (0, 0, 0))],
        out_specs=[out_spec, out_spec],
        out_shape=[jax.ShapeDtypeStruct((sub, d * GROUP_WIDTH), F32)] * 2,
        compiler_params=_params(("parallel", "arbitrary"), blocks),
        name=f"dswa_prompt_g{g}",
    )(zv, zv, zv, zv, zv, bias)
    return o.reshape(s_len, GROUP_WIDTH), lse.reshape(s_len, GROUP_WIDTH)


def _mem_attn_prompt_kernel(q_ref, kv_ref, o_ref):
    for h in range(N_MEM_HEADS):
        sl = slice(h * HEAD_DIM, (h + 1) * HEAD_DIM)
        q = q_ref[:, sl].astype(BF16)
        k = kv_ref[:, sl].astype(BF16)
        v = kv_ref[:, MEM_WIDTH + h * HEAD_DIM:MEM_WIDTH + (h + 1) * HEAD_DIM].astype(BF16)
        s = _dot_nt(q, k) * ATTN_SCALE
        m = jnp.max(s, axis=-1, keepdims=True)
        e = jnp.exp(s - m)
        l = jnp.sum(e, axis=-1, keepdims=True)
        o_ref[:, sl] = _dot(e.astype(BF16), v) / l


def mem_attn_prompt(z, col_block, mem_kv, *, tm=512):
    n = z.shape[0]
    tm = _tile(n, tm)
    blocks = 2 * _nbytes((tm, MEM_WIDTH), F32) + _nbytes(mem_kv.shape, F32)
    return pl.pallas_call(
        _mem_attn_prompt_kernel,
        grid=(n // tm,),
        in_specs=[pl.BlockSpec((tm, MEM_WIDTH), lambda i: (i, col_block)),
                  pl.BlockSpec(mem_kv.shape, lambda i: (0, 0))],
        out_specs=pl.BlockSpec((tm, MEM_WIDTH), lambda i: (i, 0)),
        out_shape=jax.ShapeDtypeStruct((n, MEM_WIDTH), F32),
        compiler_params=_params(("parallel",), blocks, _nbytes((tm, mem_kv.shape[0]), F32) * 4),
        name="mem_attn_prompt",
    )(z, mem_kv)


ONES_ROWS = 16


def _head_keys_kernel(c_ref, w_ref, kr_ref, o_ref):
    o_ref[0, :, :QK_NOPE_DIM] = _dot(c_ref[...], w_ref[...]).astype(o_ref.dtype)
    o_ref[0, :, QK_NOPE_DIM:] = kr_ref[...]


def head_keys(c, w_uk, kr, *, tm=2048):
    n, k = c.shape
    tm = _tile(n, tm)
    blocks = _nbytes((tm, k), BF16) + _nbytes((k, QK_NOPE_DIM), BF16) + 3 * _nbytes((tm, LANES), BF16)
    return pl.pallas_call(
        _head_keys_kernel,
        grid=(N_B_HEADS, n // tm),
        in_specs=[pl.BlockSpec((tm, k), lambda h, i: (i, 0)),
                  pl.BlockSpec((k, QK_NOPE_DIM), lambda h, i: (0, h)),
                  pl.BlockSpec((tm, ROPE_PAD), lambda h, i: (i, 0))],
        out_specs=pl.BlockSpec((1, tm, Q_HEAD_PAD), lambda h, i: (h, i, 0)),
        out_shape=jax.ShapeDtypeStruct((N_B_HEADS, n, Q_HEAD_PAD), BF16),
        compiler_params=_params(("parallel", "parallel"), blocks),
        name="head_keys",
    )(c, w_uk, kr)


def _head_values_t_kernel(w_ref, c_ref, o_ref):
    o_ref[0, :V_HEAD_DIM, :] = _dot_nt(w_ref[...], c_ref[...]).astype(o_ref.dtype)
    o_ref[0, V_HEAD_DIM:, :] = jnp.ones((ONES_ROWS, o_ref.shape[2]), o_ref.dtype)


def head_values_t(w_uv_t, c, *, tn=2048):
    n, k = c.shape
    tn = _tile(n, tn)
    rows = V_HEAD_DIM + ONES_ROWS
    blocks = _nbytes((V_HEAD_DIM, k), BF16) + _nbytes((tn, k), BF16) + _nbytes((rows, tn), BF16)
    return pl.pallas_call(
        _head_values_t_kernel,
        grid=(N_B_HEADS, n // tn),
        in_specs=[pl.BlockSpec((V_HEAD_DIM, k), lambda h, j: (h, 0)), pl.BlockSpec((tn, k), lambda h, j: (j, 0))],
        out_specs=pl.BlockSpec((1, rows, tn), lambda h, j: (h, 0, j)),
        out_shape=jax.ShapeDtypeStruct((N_B_HEADS, rows, n), BF16),
        compiler_params=_params(("parallel", "parallel"), blocks),
        name="head_values_t",
    )(w_uv_t, c)


def _mla_prompt_kernel(q_ref, kcat_ref, vt1_ref, o_ref, *, tq, hps):
    qi = pl.program_id(1)
    nv = V_HEAD_DIM
    qs = [q_ref[:, a * Q_HEAD_PAD:(a + 1) * Q_HEAD_PAD] for a in range(hps)]

    def step(kb, carry, masked):
        start = pl.multiple_of(kb * tq, tq)
        scores = [_dot_nt(kcat_ref[a, pl.ds(start, tq), :], qs[a]) for a in range(hps)]
        out = []
        for a in range(hps):
            m, acc = carry[a]
            s = scores[a]
            if masked:
                key = lax.broadcasted_iota(jnp.int32, (tq, tq), 0)
                qry = lax.broadcasted_iota(jnp.int32, (tq, tq), 1)
                s = jnp.where(key <= qry, s, NEG)
            m_new = jnp.maximum(m, jnp.max(s, axis=0, keepdims=True))
            alpha = jnp.exp(m - m_new)
            p = jnp.exp(s - m_new).astype(BF16)
            acc = alpha * acc + _dot(vt1_ref[a, :, pl.ds(start, tq)], p)
            out.append((m_new, acc))
        return tuple(out)

    init = tuple((jnp.full((1, tq), NEG, F32), jnp.zeros((vt1_ref.shape[1], tq), F32)) for _ in range(hps))
    carry = lax.fori_loop(0, qi, lambda kb, c: step(kb, c, False), init)
    carry = step(qi, carry, True)
    for a, (_, acc) in enumerate(carry):
        o_ref[:, a * nv:(a + 1) * nv] = (acc[:nv] / acc[nv:nv + 1]).T.astype(o_ref.dtype)


def mla_prompt(q, kcat, vt1, *, tq=512, heads_per_step=4):
    s_len = q.shape[0]
    tq = _tile(s_len, tq)
    hps = heads_per_step
    vrows = vt1.shape[1]
    blocks = _nbytes((tq, hps * Q_HEAD_PAD), BF16) + _nbytes((tq, hps * V_HEAD_DIM), BF16)
    resident = (hps * _nbytes((s_len, Q_HEAD_PAD), BF16) + hps * _nbytes((vrows, s_len), BF16)) // 2
    return pl.pallas_call(
        functools.partial(_mla_prompt_kernel, tq=tq, hps=hps),
        grid=(N_B_HEADS // hps, s_len // tq),
        in_specs=[pl.BlockSpec((tq, hps * Q_HEAD_PAD), lambda h, i: (i, h)),
                  pl.BlockSpec((hps, s_len, Q_HEAD_PAD), lambda h, i: (h, 0, 0), pipeline_mode=pl.Buffered(1)),
                  pl.BlockSpec((hps, vrows, s_len), lambda h, i: (h, 0, 0), pipeline_mode=pl.Buffered(1))],
        out_specs=pl.BlockSpec((tq, hps * V_HEAD_DIM), lambda h, i: (i, h)),
        out_shape=jax.ShapeDtypeStruct((s_len, N_B_HEADS * V_HEAD_DIM), BF16),
        compiler_params=_params(("parallel", "arbitrary"), blocks + 4 * hps * _nbytes((tq, tq), F32), resident),
        name="mla_prompt",
    )(q, kcat, vt1)


def _row_scores(q, k, bias):
    return jnp.sum(k * q[None], axis=-1, keepdims=True) * ATTN_SCALE + bias


def _dswa_sample_kernel(buf_ref, new_ref, q_ref, bias_a_ref, bias_b_ref, nbuf_ref, o_ref, lse_ref, *, d):
    nsub = buf_ref.shape[1]
    nh = HEADS_PER_GROUP
    t_new = new_ref.shape[1]
    for r_out in range(d):
        sh, r_in = divmod(r_out + t_new, d)
        if nsub - sh > 0:
            nbuf_ref[0, 0:nsub - sh, r_out] = buf_ref[0, sh:nsub, r_in]
        for a in range(max(nsub - sh, 0), nsub):
            nbuf_ref[0, a, r_out] = new_ref[0, a * d + r_out + t_new - nsub * d]
    k_new = new_ref[0, :, 0:nh, :]
    v_new = new_ref[0, :, nh:2 * nh, :]
    for i in range(t_new):
        q = q_ref[0, i]
        k_old = buf_ref[0, :, i % d, 0:nh, :]
        v_old = buf_ref[0, :, i % d, nh:2 * nh, :]
        s_a = _row_scores(q, k_old, bias_a_ref[i])
        s_b = _row_scores(q, k_new, bias_b_ref[i])
        m = jnp.maximum(jnp.max(s_a, axis=0), jnp.max(s_b, axis=0))
        e_a = jnp.exp(s_a - m[None])
        e_b = jnp.exp(s_b - m[None])
        l = jnp.sum(e_a, axis=0) + jnp.sum(e_b, axis=0)
        acc = jnp.sum(e_a * v_old, axis=0) + jnp.sum(e_b * v_new, axis=0)
        o_ref[0, i] = acc / l
        lse_ref[0, i] = m + jnp.log(l)


def dswa_sample(buf, new_kv, q, bias_a, bias_b, d):
    b, l_buf = buf.shape[:2]
    t_new = new_kv.shape[1]
    nsub = l_buf // d
    rows = 2 * HEADS_PER_GROUP
    bufv = buf.reshape(b, nsub, d, rows, HEAD_DIM)
    buf_spec = pl.BlockSpec((1, nsub, d, rows, HEAD_DIM), lambda i: (i, 0, 0, 0, 0))
    tok = lambda r: pl.BlockSpec((1, t_new, r, HEAD_DIM), lambda i: (i, 0, 0, 0))
    full = lambda a: pl.BlockSpec(a.shape, lambda i: (0,) * a.ndim)
    blocks = 2 * _nbytes((l_buf, rows, HEAD_DIM), F32) + _nbytes(bias_a.shape, F32)
    nbuf, o, lse = pl.pallas_call(
        functools.partial(_dswa_sample_kernel, d=d),
        grid=(b,),
        in_specs=[buf_spec, tok(rows), tok(HEADS_PER_GROUP), full(bias_a), full(bias_b)],
        out_specs=[buf_spec, tok(HEADS_PER_GROUP), tok(HEADS_PER_GROUP)],
        out_shape=[jax.ShapeDtypeStruct(bufv.shape, F32),
                   jax.ShapeDtypeStruct((b, t_new, HEADS_PER_GROUP, HEAD_DIM), F32),
                   jax.ShapeDtypeStruct((b, t_new, HEADS_PER_GROUP, HEAD_DIM), F32)],
        compiler_params=_params(("parallel",), blocks),
        name=f"dswa_sample_d{d}",
    )(bufv, new_kv, q, bias_a, bias_b)
    return nbuf.reshape(buf.shape), o, lse


def _dswa_sample_dense_kernel(buf_ref, new_ref, q_ref, bias_ref, nbuf_ref, o_ref, lse_ref):
    nh = HEADS_PER_GROUP
    l_buf, rows = buf_ref.shape[1], buf_ref.shape[2]
    t_new = new_ref.shape[1]
    ncol = (l_buf + t_new) * rows
    data, scores = [], []
    for j in range(q_ref.shape[0]):
        x, fresh = buf_ref[j], new_ref[j]
        nbuf_ref[j, 0:l_buf - t_new] = x[t_new:]
        nbuf_ref[j, l_buf - t_new:] = fresh
        src = jnp.concatenate([x, fresh], axis=0)
        swapped = pltpu.roll(src, nh, axis=1).reshape(ncol, HEAD_DIM).astype(BF16)
        data.append(src.reshape(ncol, HEAD_DIM).astype(BF16))
        scores.append(_dot_nt(q_ref[j].astype(BF16), swapped) * ATTN_SCALE + bias_ref[...])
    for j, s in enumerate(scores):
        m = jnp.max(s, axis=-1, keepdims=True)
        e = jnp.exp(s - m)
        l = jnp.sum(e, axis=-1, keepdims=True)
        o_ref[j] = _dot(e.astype(BF16), data[j]) / l
        lse_ref[j] = jnp.broadcast_to(m + jnp.log(l), o_ref.shape[1:])


def dswa_sample_dense(buf, new_kv, q, bias):
    b, l_buf, rows, _ = buf.shape
    seq_bytes = _nbytes((l_buf, rows, HEAD_DIM), F32)
    nb = max(1, min(8, (2 << 20) // seq_bytes))
    nb = nb if b % nb == 0 else 1
    lead = lambda a: pl.BlockSpec((nb,) + a.shape[1:], lambda i: (i,) + (0,) * (a.ndim - 1))
    blocks = 2 * nb * seq_bytes + _nbytes(bias.shape, F32)
    return pl.pallas_call(
        _dswa_sample_dense_kernel,
        grid=(b // nb,),
        in_specs=[lead(buf), lead(new_kv), lead(q), pl.BlockSpec(bias.shape, lambda i: (0, 0))],
        out_specs=[lead(buf), lead(q), lead(q)],
        out_shape=[jax.ShapeDtypeStruct(buf.shape, F32), jax.ShapeDtypeStruct(q.shape, F32),
                   jax.ShapeDtypeStruct(q.shape, F32)],
        compiler_params=_params(("parallel",), blocks, 2 * nb * seq_bytes),
        name=f"dswa_sample_dense_l{l_buf}",
    )(buf, new_kv, q, bias)


def _mem_attn_sample_kernel(kv_ref, q_ref, o_ref):
    nh = N_MEM_HEADS
    nq = q_ref.shape[1]
    n_mem, rows = kv_ref.shape[2], kv_ref.shape[3]
    ncol = n_mem * rows
    col_row = lax.broadcasted_iota(jnp.int32, (nq, ncol), 1) % rows
    q_head = lax.broadcasted_iota(jnp.int32, (nq, ncol), 0) // (nq // nh)
    own = col_row == q_head + nh
    data, scores = [], []
    for j in range(q_ref.shape[0]):
        x = kv_ref[0, j]
        swapped = pltpu.roll(x, nh, axis=1).reshape(ncol, HEAD_DIM).astype(BF16)
        data.append(x.reshape(ncol, HEAD_DIM).astype(BF16))
        scores.append(_dot_nt(q_ref[j].astype(BF16), swapped) * ATTN_SCALE)
    for j, s in enumerate(scores):
        s = jnp.where(own, s, NEG)
        e = jnp.exp(s - jnp.max(s, axis=-1, keepdims=True))
        o_ref[j] = _dot(e.astype(BF16), data[j]) / jnp.sum(e, axis=-1, keepdims=True)


def mem_attn_sample(cache, layer, q, *, seqs_per_step=4):
    _, b, n_mem, rows, _ = cache.shape
    nb = seqs_per_step if b % seqs_per_step == 0 else 1
    tok = pl.BlockSpec((nb,) + q.shape[1:], lambda i: (i, 0, 0))
    return pl.pallas_call(
        _mem_attn_sample_kernel,
        grid=(b // nb,),
        in_specs=[pl.BlockSpec((1, nb, n_mem, rows, HEAD_DIM), lambda i: (layer, i, 0, 0, 0)), tok],
        out_specs=tok,
        out_shape=jax.ShapeDtypeStruct(q.shape, F32),
        compiler_params=_params(("parallel",), nb * _nbytes((n_mem, rows, HEAD_DIM), F32),
                                2 * nb * _nbytes((n_mem, rows, HEAD_DIM), F32)),
        name="mem_attn_sample",
    )(cache, q)


def _mla_sample_kernel(pt_ref, q_ref, new_ref, *rest, n_pages, n_seq, t_new):
    del pt_ref
    page_refs = rest[:n_pages * n_seq]
    o_ref, m_ref, l_ref, acc_ref = rest[n_pages * n_seq:]
    c = pl.program_id(1)
    nq = q_ref.shape[1]

    @pl.when(c == 0)
    def _():
        m_ref[...] = jnp.full(m_ref.shape, NEG, F32)
        l_ref[...] = jnp.zeros(l_ref.shape, F32)
        acc_ref[...] = jnp.zeros(acc_ref.shape, F32)

    def update(j, s_parts, pv_fn):
        m_prev = m_ref[j]
        m_cur = jnp.max(functools.reduce(jnp.maximum, s_parts), axis=-1, keepdims=True)
        m_new = jnp.maximum(m_prev, m_cur)
        alpha = jnp.exp(m_prev - m_new)
        p = [jnp.exp(s - m_new) for s in s_parts]
        l_ref[j] = alpha * l_ref[j] + jnp.sum(functools.reduce(jnp.add, p), axis=-1, keepdims=True)
        m_ref[j] = m_new
        acc_ref[j] = alpha * acc_ref[j] + pv_fn([x.astype(BF16) for x in p])

    pages = [[r[0].astype(BF16) for r in page_refs[j * n_pages:(j + 1) * n_pages]] for j in range(n_seq)]
    scores = [[_dot(q_ref[j], kt) for kt in pages[j]] for j in range(n_seq)]
    for j in range(n_seq):
        update(j, scores[j], lambda p, kts=pages[j]: functools.reduce(
            jnp.add, [_dot_nt(pi, kt[:KV_LORA_RANK]) for pi, kt in zip(p, kts)]))

    @pl.when(c == pl.num_programs(1) - 1)
    def _():
        tok = lax.broadcasted_iota(jnp.int32, (nq, t_new), 0) // (nq // t_new)
        key = lax.broadcasted_iota(jnp.int32, (nq, t_new), 1)
        for j in range(n_seq):
            nr = new_ref[j].astype(BF16)
            s = jnp.where(key <= tok, _dot_nt(q_ref[j], nr), NEG)
            update(j, [s], lambda p, nr=nr: _dot(p[0], nr[:, :KV_LORA_RANK]))
            o_ref[j] = acc_ref[j] / l_ref[j]


def mla_sample(q, new_rows, cache_t, page_table):
    b, nq, row = q.shape
    t_new = new_rows.shape[1]
    n_seq_pages = page_table.shape[1]
    pps = PAGES_PER_STEP if n_seq_pages % PAGES_PER_STEP == 0 else 1
    nb = SEQS_PER_STEP if b % SEQS_PER_STEP == 0 else 1

    def page_spec(j, p):
        return pl.BlockSpec((1, row, PAGE_SIZE), lambda i, c, pt: (pt[i * nb + j, c * pps + p], 0, 0))

    blocks = (nb * pps * _nbytes((row, PAGE_SIZE), F32) + nb * _nbytes((nq, row), BF16)
              + nb * _nbytes((nq, KV_LORA_RANK), F32))
    return pl.pallas_call(
        functools.partial(_mla_sample_kernel, n_pages=pps, n_seq=nb, t_new=t_new),
        grid_spec=pltpu.PrefetchScalarGridSpec(
            num_scalar_prefetch=1,
            grid=(b // nb, n_seq_pages // pps),
            in_specs=[pl.BlockSpec((nb, nq, row), lambda i, c, pt: (i, 0, 0)),
                      pl.BlockSpec((nb, t_new, row), lambda i, c, pt: (i, 0, 0))]
                     + [page_spec(j, p) for j in range(nb) for p in range(pps)],
            out_specs=pl.BlockSpec((nb, nq, KV_LORA_RANK), lambda i, c, pt: (i, 0, 0)),
            scratch_shapes=[pltpu.VMEM((nb, nq, 1), F32), pltpu.VMEM((nb, nq, 1), F32),
                            pltpu.VMEM((nb, nq, KV_LORA_RANK), F32)]),
        out_shape=jax.ShapeDtypeStruct((b, nq, KV_LORA_RANK), F32),
        compiler_params=_params(("parallel", "arbitrary"), blocks,
                                nb * pps * _nbytes((row, PAGE_SIZE), BF16)),
        name="mla_sample",
    )(page_table, q, new_rows, *([cache_t] * (nb * pps)))


def _t5_bucket(dist):
    max_exact = N_BUCKETS // 2
    dd = jnp.maximum(dist, 1).astype(F32)
    large = max_exact + (jnp.log(dd / max_exact) / math.log(MAX_DISTANCE / max_exact)
                         * (N_BUCKETS - max_exact)).astype(jnp.int32)
    large = jnp.minimum(large, N_BUCKETS - 1)
    return jnp.where(dist < max_exact, dist, large)


def _group_bias(t5_bias, g):
    w, d = DSWA_GROUPS[g]
    offs = jnp.arange(w // d + 1, dtype=jnp.int32) * d
    return t5_bias[_t5_bucket(offs), g * HEADS_PER_GROUP:(g + 1) * HEADS_PER_GROUP].astype(F32)


def _banded(table, j):
    jmax = table.shape[0] - 1
    vals = table[jnp.clip(j, 0, jmax)]
    return jnp.where(((j >= 0) & (j <= jmax))[..., None], vals, NEG)


def _prompt_bias(table):
    blk, nh = table.shape[0] - 1, table.shape[1]
    n = 3 * blk - 1
    pad = jnp.full((blk - 1, nh), NEG, F32)
    u = jnp.concatenate([pad, table, pad], axis=0)
    skew = jnp.tile(u, (blk + 1, 1))[:blk * (n + 1)].reshape(blk, n + 1, nh)
    return jnp.moveaxis(skew[:, :2 * blk][:, ::-1], -1, 0)


def _sample_bias(table, d, t_new):
    blk = table.shape[0] - 1
    i = jnp.arange(t_new)
    j_a = blk - jnp.arange(blk)[None, :] + (i // d)[:, None]
    diff = i[:, None] - i[None, :]
    j_b = jnp.where((diff >= 0) & (diff % d == 0), diff // d, -1)
    lanes = lambda x: jnp.broadcast_to(x[..., None], x.shape + (HEAD_DIM,))
    return lanes(_banded(table, j_a)), lanes(_banded(table, j_b))


def _sample_bias_dense(table, d, t_new):
    blk, nh = table.shape[0] - 1, table.shape[1]
    w = blk * d
    stuffed = jnp.concatenate([table[:, None, :], jnp.full((blk + 1, d - 1, nh), NEG, F32)], axis=1)
    stuffed = stuffed.reshape((blk + 1) * d, nh)[:w + 1]
    pad = jnp.full((t_new - 1, nh), NEG, F32)
    rev = jnp.concatenate([pad, stuffed, pad], axis=0)[::-1]
    band = jnp.stack([rev[t_new - 1 - i:t_new - 1 - i + w + t_new] for i in range(t_new)])
    band = jnp.transpose(band, (2, 0, 1))[..., None]
    head = jnp.arange(nh)[:, None, None, None]
    sub = jnp.arange(2 * nh)[None, None, None, :]
    return jnp.where(sub == head + nh, band, NEG).reshape(nh * t_new, (w + t_new) * 2 * nh)


def _rope_tables(pos):
    half = QK_ROPE_DIM // 2
    inv_freq = ROPE_THETA ** (-jnp.arange(half, dtype=F32) / half)
    ang = pos.astype(F32)[:, None] * inv_freq[None, :]
    cos, sin = jnp.cos(ang), jnp.sin(ang)
    zero = jnp.zeros_like(cos)
    cc = jnp.concatenate([cos, cos, zero, zero], axis=-1)
    shi = jnp.concatenate([zero, sin, zero, zero], axis=-1)
    slo = jnp.concatenate([-sin, zero, zero, zero], axis=-1)
    return cc, shi, slo


def _pad_last(w, to):
    return jnp.pad(w, [(0, 0)] * (w.ndim - 1) + [(0, to - w.shape[-1])])


def _trunk(x, rope_tables, dswa_fn, mem_fn, mla_fn, p):
    z0 = norm_matmul(x, p['g_attn_pre'][0], p['w_a_in'])
    o_tok = dswa_fn(z0)
    o_mem = mem_fn(0, z0, A_QKV_WIDTH // MEM_WIDTH)
    x = out_proj(o_tok, o_mem, p['w_a_out'][:GROUP_WIDTH], p['w_a_out'][GROUP_WIDTH:], p['g_attn_post'][0], x)
    x = mlp(x, p['g_mlp_pre'][0], p['w_mlp_up'][0], p['w_mlp_down'][0], p['g_mlp_post'][0])
    raw = norm_matmul(x, p['g_kv_in'], p['w_kv_down'])
    rows, c_bf, kr_bf = kv_finish(raw, p['g_kv_latent'], rope_tables)
    z = norm_matmul(x, p['g_attn_pre'][1], p['w_b_in'])
    q = norm_matmul(z, p['g_q_latent'], p['w_q_up'], k_width=Q_LORA_RANK, rope_tables=rope_tables,
                    scale=MLA_SCALE, out_dtype=BF16, tn=2 * Q_HEAD_PAD)
    o_tok = mla_fn(q, rows, c_bf, kr_bf)
    o_mem = mem_fn(1, z, Q_LORA_RANK // MEM_WIDTH)
    nv = N_B_HEADS * V_HEAD_DIM
    x = out_proj(o_tok, o_mem, p['w_b_out'][:nv], p['w_b_out'][nv:], p['g_attn_post'][1], x)
    x = mlp(x, p['g_mlp_pre'][1], p['w_mlp_up'][1], p['w_mlp_down'][1], p['g_mlp_post'][1])
    return x, rows[:, :MLA_ROW], z0


def kernel(x_prompt, x_sample, mem_prompt, cache_swa_kv_w128, cache_swa_kv_w512, cache_swa_kv_w2048, cache_mla_kv, cache_mem_kv, page_table, t5_bias, g_attn_pre, g_attn_post, g_mlp_pre, g_mlp_post, g_mem, w_mem_kv, w_mlp_up, w_mlp_down, w_a_in, w_a_out, g_kv_in, w_kv_down, g_kv_latent, w_kv_up, w_b_in, g_q_latent, w_q_up, w_b_out):
    depth = g_attn_pre.shape[0]
    assert depth == 2 and w_a_in.shape[0] == 1 and w_b_in.shape[0] == 1
    bp, s_len, d_model = x_prompt.shape
    assert bp == 1
    bs, t_new, _ = x_sample.shape
    past = page_table.shape[1] * PAGE_SIZE
    bufs = (cache_swa_kv_w128, cache_swa_kv_w512, cache_swa_kv_w2048)
    for buf, (w, d) in zip(bufs, DSWA_GROUPS):
        assert buf.shape[2] == w and past >= w and s_len % (d * (w // d)) == 0 and s_len >= w

    w_q_up_pad = _pad_last(w_q_up[0].reshape(Q_LORA_RANK, N_B_HEADS, QK_NOPE_DIM + QK_ROPE_DIM),
                           Q_HEAD_PAD).reshape(Q_LORA_RANK, N_B_HEADS * Q_HEAD_PAD)
    p = {
        'g_attn_pre': g_attn_pre, 'g_attn_post': g_attn_post, 'g_mlp_pre': g_mlp_pre, 'g_mlp_post': g_mlp_post,
        'g_kv_in': g_kv_in, 'g_kv_latent': g_kv_latent, 'g_q_latent': g_q_latent[0],
        'w_a_in': w_a_in[0].astype(BF16), 'w_a_out': w_a_out[0].astype(BF16),
        'w_b_in': w_b_in[0].astype(BF16), 'w_b_out': w_b_out[0].astype(BF16),
        'w_mlp_up': w_mlp_up.astype(BF16), 'w_mlp_down': w_mlp_down.astype(BF16),
        'w_kv_down': _pad_last(w_kv_down, MLA_ROW_PAD).astype(BF16),
        'w_q_up': w_q_up_pad.astype(BF16),
    }
    w_uk = w_kv_up[..., :QK_NOPE_DIM]
    w_uv = w_kv_up[..., QK_NOPE_DIM:]
    w_uk_cat = w_uk.reshape(KV_LORA_RANK, -1).astype(BF16)
    w_uv_cat_t = w_uv.reshape(KV_LORA_RANK, -1).T.astype(BF16)
    w_uk_t = jnp.transpose(w_uk, (1, 2, 0)).astype(BF16)
    w_uv_h = jnp.transpose(w_uv, (1, 0, 2)).astype(BF16)
    tables = [_group_bias(t5_bias, g) for g in range(N_GROUPS)]

    mem_kv_prompt = [norm_matmul(mem_prompt[0], g_mem[l], w_mem_kv[l].astype(BF16)) for l in range(depth)]

    def dswa_p(z):
        parts = [dswa_prompt(z, g, _prompt_bias(tables[g])) for g in range(N_GROUPS)]
        return combine_groups([o for o, _ in parts], [l for _, l in parts])

    def mla_p(q, rows, c_bf, kr_bf):
        return mla_prompt(q, head_keys(c_bf, w_uk_cat, kr_bf), head_values_t(w_uv_cat_t, c_bf))

    y_p, rows_p, z_p = _trunk(x_prompt[0], _rope_tables(jnp.arange(s_len, dtype=jnp.int32)), dswa_p,
                              lambda l, z, cb: mem_attn_prompt(z, cb, mem_kv_prompt[l]), mla_p, p)
    swa_p = []
    for g, (w, _) in enumerate(DSWA_GROUPS):
        k_g = z_p[s_len - w:, (N_GROUPS + g) * GROUP_WIDTH:(N_GROUPS + g + 1) * GROUP_WIDTH]
        v_g = z_p[s_len - w:, (2 * N_GROUPS + g) * GROUP_WIDTH:(2 * N_GROUPS + g + 1) * GROUP_WIDTH]
        kv_g = jnp.stack([k_g.reshape(w, HEADS_PER_GROUP, HEAD_DIM), v_g.reshape(w, HEADS_PER_GROUP, HEAD_DIM)], axis=1)
        swa_p.append(kv_g[None, None])
    mem_kv_out = jnp.stack(mem_kv_prompt).reshape(depth, 1, mem_prompt.shape[1], 2, N_MEM_HEADS, HEAD_DIM)

    ns = bs * t_new
    pos_s = past + (jnp.arange(ns, dtype=jnp.int32) % t_new)
    swa_s = []

    def heads(a, nh):
        return a.reshape(bs, t_new, nh, HEAD_DIM)

    def dswa_s(z):
        outs, lses = [], []
        for g, (w, d) in enumerate(DSWA_GROUPS):
            sec = lambda c: z[:, (c * N_GROUPS + g) * GROUP_WIDTH:(c * N_GROUPS + g + 1) * GROUP_WIDTH]
            new_kv = jnp.concatenate([heads(sec(1), HEADS_PER_GROUP), heads(sec(2), HEADS_PER_GROUP)], axis=2)
            buf = bufs[g][0].reshape(bs, w, 2 * HEADS_PER_GROUP, HEAD_DIM)
            q = heads(sec(0), HEADS_PER_GROUP)
            if w <= DENSE_SAMPLE_MAX_ROWS:
                q = jnp.swapaxes(q, 1, 2).reshape(bs, HEADS_PER_GROUP * t_new, HEAD_DIM)
                nbuf, o, lse = dswa_sample_dense(buf, new_kv, q, _sample_bias_dense(tables[g], d, t_new))
                o, lse = (jnp.swapaxes(a.reshape(bs, HEADS_PER_GROUP, t_new, HEAD_DIM), 1, 2) for a in (o, lse))
            else:
                bias_a, bias_b = _sample_bias(tables[g], d, t_new)
                nbuf, o, lse = dswa_sample(buf, new_kv, q, bias_a, bias_b, d)
            swa_s.append(nbuf.reshape(bufs[g].shape))
            outs.append(o.reshape(ns, GROUP_WIDTH))
            lses.append(lse.reshape(ns, GROUP_WIDTH))
        return combine_groups(outs, lses)

    cache_mem = cache_mem_kv.reshape(depth, bs, cache_mem_kv.shape[2], 2 * N_MEM_HEADS, HEAD_DIM)

    def mem_s(l, z, cb):
        q = heads(z[:, cb * MEM_WIDTH:(cb + 1) * MEM_WIDTH], N_MEM_HEADS)
        q = jnp.swapaxes(q, 1, 2).reshape(bs, N_MEM_HEADS * t_new, HEAD_DIM)
        o = mem_attn_sample(cache_mem, l, q).reshape(bs, N_MEM_HEADS, t_new, HEAD_DIM)
        return jnp.swapaxes(o, 1, 2).reshape(ns, MEM_WIDTH)

    cache_mla_t = jnp.swapaxes(cache_mla_kv, 1, 2)

    def mla_s(q, rows, c_bf, kr_bf):
        q_lat = head_matmul(q, w_uk_t, x_block_stride=Q_HEAD_PAD // QK_NOPE_DIM, out_dtype=BF16)
        q3 = q.reshape(ns, N_B_HEADS, Q_HEAD_PAD)
        q_full = jnp.concatenate([q_lat.reshape(ns, N_B_HEADS, KV_LORA_RANK),
                                  q3[:, :, QK_NOPE_DIM:QK_NOPE_DIM + QK_ROPE_DIM]], axis=-1)
        q_full = q_full.reshape(bs, t_new * N_B_HEADS, MLA_ROW)
        new_rows = rows[:, :MLA_ROW].reshape(bs, t_new, MLA_ROW)
        o_lat = mla_sample(q_full, new_rows, cache_mla_t, page_table)
        return head_matmul(o_lat.reshape(ns, N_B_HEADS * KV_LORA_RANK), w_uv_h, out_dtype=BF16)

    y_s, rows_s, _ = _trunk(x_sample.reshape(ns, d_model), _rope_tables(pos_s), dswa_s, mem_s, mla_s, p)

    return (y_p[None], y_s.reshape(bs, t_new, d_model), swa_p[0], swa_p[1], swa_p[2],
            rows_p[None], mem_kv_out, swa_s[0], swa_s[1], swa_s[2], rows_s.reshape(bs, t_new, MLA_ROW))
```

```python
import functools
import math

import jax
import jax.numpy as jnp
from jax import lax
from jax.experimental import pallas as pl
from jax.experimental.pallas import tpu as pltpu

F32 = jnp.float32
BF16 = jnp.bfloat16

HEAD_DIM = 128
DSWA_GROUPS = ((128, 1), (512, 4), (2048, 16))
N_GROUPS = len(DSWA_GROUPS)
HEADS_PER_GROUP = 4
GROUP_WIDTH = HEADS_PER_GROUP * HEAD_DIM
N_A_HEADS = N_GROUPS * HEADS_PER_GROUP
A_QKV_WIDTH = 3 * N_A_HEADS * HEAD_DIM
N_MEM_HEADS = 4
MEM_WIDTH = N_MEM_HEADS * HEAD_DIM
N_B_HEADS = 12
Q_LORA_RANK = 1536
KV_LORA_RANK = 512
QK_NOPE_DIM = 128
QK_ROPE_DIM = 64
V_HEAD_DIM = 128
MLA_ROW = KV_LORA_RANK + QK_ROPE_DIM
MLA_SCALE = (QK_NOPE_DIM + QK_ROPE_DIM) ** -0.5
ATTN_SCALE = HEAD_DIM ** -0.5
ROPE_THETA = 10000.0
N_BUCKETS = 32
MAX_DISTANCE = 2048
EPS = 1e-6
PAGE_SIZE = 128

LANES = 128
ROPE_PAD = LANES
Q_HEAD_PAD = QK_NOPE_DIM + ROPE_PAD
MLA_ROW_PAD = KV_LORA_RANK + ROPE_PAD
NEG = -1e30
VMEM_BYTES_V7X = 64 * 1024 * 1024
VMEM_CAP = VMEM_BYTES_V7X - 8 * 1024 * 1024
PAGES_PER_STEP = 16
SEQS_PER_STEP = 4
DENSE_SAMPLE_MAX_ROWS = 512


def _tile(n, pref):
    return pref if n % pref == 0 else n


def _nbytes(shape, dtype):
    return math.prod(shape) * jnp.dtype(dtype).itemsize


def _params(semantics, pipelined_bytes, resident_bytes=0):
    est = 2 * pipelined_bytes + 2 * resident_bytes + (4 << 20)
    return pltpu.CompilerParams(dimension_semantics=semantics,
                                vmem_limit_bytes=int(min(max(est, 16 << 20), VMEM_CAP)))


def _rms(x, g):
    return x * lax.rsqrt(jnp.mean(x * x, axis=-1, keepdims=True) + EPS) * g


def _rope_pad(r, cc, shi, slo):
    half = QK_ROPE_DIM // 2
    return r * cc + pltpu.roll(r, half, axis=1) * shi + pltpu.roll(r, LANES - half, axis=1) * slo


def _dot(a, b):
    return jnp.dot(a, b, preferred_element_type=F32)


def _dot_nt(a, b):
    return lax.dot_general(a, b, (((1,), (1,)), ((), ())), preferred_element_type=F32)


def _norm_matmul_kernel(*refs, norm, rope, scale):
    if rope:
        x_ref, g_ref, w_ref, cc_ref, shi_ref, slo_ref, o_ref, xn_ref = refs
    else:
        x_ref, g_ref, w_ref, o_ref, xn_ref = refs

    @pl.when(pl.program_id(1) == 0)
    def _():
        x = x_ref[...].astype(F32)
        if norm:
            x = _rms(x, g_ref[...])
        xn_ref[...] = x.astype(BF16)

    acc = _dot(xn_ref[...], w_ref[...])
    if scale != 1.0:
        acc = acc * scale
    if rope:
        cc, shi, slo = cc_ref[...], shi_ref[...], slo_ref[...]
        for h in range(acc.shape[1] // Q_HEAD_PAD):
            lo = h * Q_HEAD_PAD
            o_ref[:, lo:lo + QK_NOPE_DIM] = acc[:, lo:lo + QK_NOPE_DIM].astype(o_ref.dtype)
            r = _rope_pad(acc[:, lo + QK_NOPE_DIM:lo + Q_HEAD_PAD], cc, shi, slo)
            o_ref[:, lo + QK_NOPE_DIM:lo + Q_HEAD_PAD] = r.astype(o_ref.dtype)
    else:
        o_ref[...] = acc.astype(o_ref.dtype)


def norm_matmul(x, g, w, *, k_width=None, norm=True, rope_tables=None, scale=1.0,
                out_dtype=F32, tm=1024, tn=512):
    n = x.shape[0]
    k = k_width or x.shape[1]
    nout = w.shape[1]
    tm = _tile(n, tm)
    tn = _tile(nout, tn)
    rope = rope_tables is not None
    in_specs = [pl.BlockSpec((tm, k), lambda i, j: (i, 0)),
                pl.BlockSpec((1, k), lambda i, j: (0, 0)),
                pl.BlockSpec((k, tn), lambda i, j: (0, j))]
    args = [x, g.reshape(1, k).astype(F32), w]
    if rope:
        in_specs += [pl.BlockSpec((tm, LANES), lambda i, j: (i, 0))] * 3
        args += list(rope_tables)
    blocks = (_nbytes((tm, k), x.dtype) + _nbytes((k, tn), w.dtype) + _nbytes((tm, tn), out_dtype)
              + 3 * _nbytes((tm, LANES), F32))
    return pl.pallas_call(
        functools.partial(_norm_matmul_kernel, norm=norm, rope=rope, scale=scale),
        grid=(n // tm, nout // tn),
        in_specs=in_specs,
        out_specs=pl.BlockSpec((tm, tn), lambda i, j: (i, j)),
        out_shape=jax.ShapeDtypeStruct((n, nout), out_dtype),
        scratch_shapes=[pltpu.VMEM((tm, k), BF16)],
        compiler_params=_params(("parallel", "arbitrary"), blocks,
                                _nbytes((tm, k), F32) + _nbytes((tm, tn), F32)),
        name="norm_matmul",
    )(*args)


def _kv_finish_kernel(raw_ref, g_ref, cc_ref, shi_ref, slo_ref, rows_ref, c_ref, kr_ref):
    c = _rms(raw_ref[:, :KV_LORA_RANK], g_ref[...])
    kr = _rope_pad(raw_ref[:, KV_LORA_RANK:], cc_ref[...], shi_ref[...], slo_ref[...])
    rows_ref[:, :KV_LORA_RANK] = c
    rows_ref[:, KV_LORA_RANK:] = kr
    c_ref[...] = c.astype(BF16)
    kr_ref[...] = kr.astype(BF16)


def kv_finish(raw, g_latent, rope_tables, *, tm=512):
    n = raw.shape[0]
    tm = _tile(n, tm)
    row = lambda w: pl.BlockSpec((tm, w), lambda i: (i, 0))
    blocks = 2 * _nbytes((tm, MLA_ROW_PAD), F32) + 4 * _nbytes((tm, LANES), F32) + _nbytes((tm, KV_LORA_RANK), BF16)
    return pl.pallas_call(
        _kv_finish_kernel,
        grid=(n // tm,),
        in_specs=[row(MLA_ROW_PAD), pl.BlockSpec((1, KV_LORA_RANK), lambda i: (0, 0)),
                  row(LANES), row(LANES), row(LANES)],
        out_specs=[row(MLA_ROW_PAD), row(KV_LORA_RANK), row(LANES)],
        out_shape=[jax.ShapeDtypeStruct((n, MLA_ROW_PAD), F32),
                   jax.ShapeDtypeStruct((n, KV_LORA_RANK), BF16),
                   jax.ShapeDtypeStruct((n, LANES), BF16)],
        compiler_params=_params(("parallel",), blocks),
        name="kv_finish",
    )(raw, g_latent.reshape(1, -1).astype(F32), *rope_tables)


def _out_proj_kernel(a1_ref, a2_ref, w1_ref, w2_ref, g_ref, x_ref, o_ref):
    o = _dot(a1_ref[...].astype(BF16), w1_ref[...]) + _dot(a2_ref[...].astype(BF16), w2_ref[...])
    o_ref[...] = x_ref[...] + _rms(o, g_ref[...])


def out_proj(a1, a2, w1, w2, g, x, *, tm=512):
    n, d = x.shape
    k1, k2 = a1.shape[1], a2.shape[1]
    tm = _tile(n, tm)
    blocks = (_nbytes((tm, k1), a1.dtype) + _nbytes((tm, k2), a2.dtype) + _nbytes((k1 + k2, d), BF16)
              + 2 * _nbytes((tm, d), F32))
    return pl.pallas_call(
        _out_proj_kernel,
        grid=(n // tm,),
        in_specs=[pl.BlockSpec((tm, k1), lambda i: (i, 0)), pl.BlockSpec((tm, k2), lambda i: (i, 0)),
                  pl.BlockSpec((k1, d), lambda i: (0, 0)), pl.BlockSpec((k2, d), lambda i: (0, 0)),
                  pl.BlockSpec((1, d), lambda i: (0, 0)), pl.BlockSpec((tm, d), lambda i: (i, 0))],
        out_specs=pl.BlockSpec((tm, d), lambda i: (i, 0)),
        out_shape=jax.ShapeDtypeStruct((n, d), F32),
        compiler_params=_params(("parallel",), blocks, _nbytes((tm, d), F32)),
        name="out_proj",
    )(a1, a2, w1, w2, g.reshape(1, d).astype(F32), x)


def _mlp_kernel(x_ref, gpre_ref, wup_ref, wdn_ref, gpost_ref, o_ref, xn_ref, acc_ref):
    j = pl.program_id(1)

    @pl.when(j == 0)
    def _():
        xn_ref[...] = _rms(x_ref[...], gpre_ref[...]).astype(BF16)

    h = _dot(xn_ref[...], wup_ref[...])
    u = jnp.square(jnp.maximum(h, 0.0)).astype(BF16)
    part = _dot(u, wdn_ref[...])

    @pl.when(j == 0)
    def _():
        acc_ref[...] = part

    @pl.when(j > 0)
    def _():
        acc_ref[...] += part

    @pl.when(j == pl.num_programs(1) - 1)
    def _():
        o_ref[...] = x_ref[...] + _rms(acc_ref[...], gpost_ref[...])


def mlp(x, g_pre, w_up, w_down, g_post, *, tm=512, tf=1024):
    n, d = x.shape
    f = w_up.shape[1]
    tm = _tile(n, tm)
    tf = _tile(f, tf)
    blocks = 2 * _nbytes((tm, d), F32) + 2 * _nbytes((d, tf), BF16)
    resident = _nbytes((tm, d), BF16) + _nbytes((tm, d), F32) + _nbytes((tm, tf), F32)
    return pl.pallas_call(
        _mlp_kernel,
        grid=(n // tm, f // tf),
        in_specs=[pl.BlockSpec((tm, d), lambda i, j: (i, 0)), pl.BlockSpec((1, d), lambda i, j: (0, 0)),
                  pl.BlockSpec((d, tf), lambda i, j: (0, j)), pl.BlockSpec((tf, d), lambda i, j: (j, 0)),
                  pl.BlockSpec((1, d), lambda i, j: (0, 0))],
        out_specs=pl.BlockSpec((tm, d), lambda i, j: (i, 0)),
        out_shape=jax.ShapeDtypeStruct((n, d), F32),
        scratch_shapes=[pltpu.VMEM((tm, d), BF16), pltpu.VMEM((tm, d), F32)],
        compiler_params=_params(("parallel", "arbitrary"), blocks, resident),
        name="mlp",
    )(x, g_pre.reshape(1, d).astype(F32), w_up, w_down, g_post.reshape(1, d).astype(F32))


def _head_matmul_kernel(x_ref, w_ref, o_ref):
    o_ref[...] = _dot(x_ref[...].astype(BF16), w_ref[0]).astype(o_ref.dtype)


def head_matmul(x, w, *, x_block_stride=1, out_dtype=F32):
    n = x.shape[0]
    nh, kh, nn = w.shape
    blocks = _nbytes((n, kh), x.dtype) + _nbytes((kh, nn), w.dtype) + _nbytes((n, nn), out_dtype)
    return pl.pallas_call(
        _head_matmul_kernel,
        grid=(nh,),
        in_specs=[pl.BlockSpec((n, kh), lambda h: (0, h * x_block_stride)),
                  pl.BlockSpec((1, kh, nn), lambda h: (h, 0, 0))],
        out_specs=pl.BlockSpec((n, nn), lambda h: (0, h)),
        out_shape=jax.ShapeDtypeStruct((n, nh * nn), out_dtype),
        compiler_params=_params(("parallel",), blocks),
        name="head_matmul",
    )(x, w)


def _combine_kernel(o0, o1, o2, l0, l1, l2, out_ref):
    m = jnp.maximum(jnp.maximum(l0[...], l1[...]), l2[...])
    w0, w1, w2 = jnp.exp(l0[...] - m), jnp.exp(l1[...] - m), jnp.exp(l2[...] - m)
    out_ref[...] = (w0 * o0[...] + w1 * o1[...] + w2 * o2[...]) / (w0 + w1 + w2)


def combine_groups(outs, lses, *, tm=512):
    n, w = outs[0].shape
    tm = _tile(n, tm)
    spec = pl.BlockSpec((tm, w), lambda i: (i, 0))
    return pl.pallas_call(
        _combine_kernel,
        grid=(n // tm,),
        in_specs=[spec] * 6,
        out_specs=spec,
        out_shape=jax.ShapeDtypeStruct((n, w), F32),
        compiler_params=_params(("parallel",), 7 * _nbytes((tm, w), F32)),
        name="combine_groups",
    )(*outs, *lses)


def _dswa_prompt_kernel(q_ref, kp_ref, kc_ref, vp_ref, vc_ref, bias_ref, o_ref, lse_ref):
    blk = q_ref.shape[0]
    col = lax.broadcasted_iota(jnp.int32, (blk, 2 * blk), 1)
    prev_invalid = col < jnp.where(pl.program_id(1) == 0, blk, 0)
    for h in range(HEADS_PER_GROUP):
        sl = slice(h * HEAD_DIM, (h + 1) * HEAD_DIM)
        q = q_ref[:, sl].astype(BF16)
        k = jnp.concatenate([kp_ref[:, sl], kc_ref[:, sl]], axis=0).astype(BF16)
        v = jnp.concatenate([vp_ref[:, sl], vc_ref[:, sl]], axis=0).astype(BF16)
        s = _dot_nt(q, k) * ATTN_SCALE + bias_ref[h]
        s = jnp.where(prev_invalid, NEG, s)
        m = jnp.max(s, axis=-1, keepdims=True)
        e = jnp.exp(s - m)
        l = jnp.sum(e, axis=-1, keepdims=True)
        acc = _dot(e.astype(BF16), v)
        o_ref[:, sl] = acc / l
        lse_ref[:, sl] = jnp.broadcast_to(m + jnp.log(l), (blk, HEAD_DIM))


def dswa_prompt(z, g, bias):
    s_len, zw = z.shape
    w, d = DSWA_GROUPS[g]
    blk = w // d
    sub = s_len // d
    nblk = sub // blk
    q_col, k_col, v_col = g, N_GROUPS + g, 2 * N_GROUPS + g
    if d == 1:
        zv, per_res = z, zw // GROUP_WIDTH
    else:
        zg = jnp.concatenate([z[:, c * GROUP_WIDTH:(c + 1) * GROUP_WIDTH] for c in (q_col, k_col, v_col)], axis=1)
        zv, per_res = zg.reshape(sub, d * 3 * GROUP_WIDTH), 3
        q_col, k_col, v_col = 0, 1, 2
    cur = lambda c: pl.BlockSpec((blk, GROUP_WIDTH), lambda r, u: (u, r * per_res + c))
    prev = lambda c: pl.BlockSpec((blk, GROUP_WIDTH), lambda r, u: (jnp.maximum(u - 1, 0), r * per_res + c))
    out_spec = pl.BlockSpec((blk, GROUP_WIDTH), lambda r, u: (u, r))
    blocks = 7 * _nbytes((blk, GROUP_WIDTH), F32) + _nbytes(bias.shape, F32)
    o, lse = pl.pallas_call(
        _dswa_prompt_kernel,
        grid=(d, nblk),
        in_specs=[cur(q_col), prev(k_col), cur(k_col), prev(v_col), cur(v_col),
                  pl.BlockSpec(bias.shape, lambda r, u: (0, 0, 0))],
        out_specs=[out_spec, out_spec],
        out_shape=[jax.ShapeDtypeStruct((sub, d * GROUP_WIDTH), F32)] * 2,
        compiler_params=_params(("parallel", "arbitrary"), blocks),
        name=f"dswa_prompt_g{g}",
    )(zv, zv, zv, zv, zv, bias)
    return o.reshape(s_len, GROUP_WIDTH), lse.reshape(s_len, GROUP_WIDTH)


def _mem_attn_prompt_kernel(q_ref, kv_ref, o_ref):
    for h in range(N_MEM_HEADS):
        sl = slice(h * HEAD_DIM, (h + 1) * HEAD_DIM)
        q = q_ref[:, sl].astype(BF16)
        k = kv_ref[:, sl].astype(BF16)
        v = kv_ref[:, MEM_WIDTH + h * HEAD_DIM:MEM_WIDTH + (h + 1) * HEAD_DIM].astype(BF16)
        s = _dot_nt(q, k) * ATTN_SCALE
        m = jnp.max(s, axis=-1, keepdims=True)
        e = jnp.exp(s - m)
        l = jnp.sum(e, axis=-1, keepdims=True)
        o_ref[:, sl] = _dot(e.astype(BF16), v) / l


def mem_attn_prompt(z, col_block, mem_kv, *, tm=512):
    n = z.shape[0]
    tm = _tile(n, tm)
    blocks = 2 * _nbytes((tm, MEM_WIDTH), F32) + _nbytes(mem_kv.shape, F32)
    return pl.pallas_call(
        _mem_attn_prompt_kernel,
        grid=(n // tm,),
        in_specs=[pl.BlockSpec((tm, MEM_WIDTH), lambda i: (i, col_block)),
                  pl.BlockSpec(mem_kv.shape, lambda i: (0, 0))],
        out_specs=pl.BlockSpec((tm, MEM_WIDTH), lambda i: (i, 0)),
        out_shape=jax.ShapeDtypeStruct((n, MEM_WIDTH), F32),
        compiler_params=_params(("parallel",), blocks, _nbytes((tm, mem_kv.shape[0]), F32) * 4),
        name="mem_attn_prompt",
    )(z, mem_kv)


ONES_ROWS = 16


def _head_keys_kernel(c_ref, w_ref, kr_ref, o_ref):
    o_ref[0, :, :QK_NOPE_DIM] = _dot(c_ref[...], w_ref[...]).astype(o_ref.dtype)
    o_ref[0, :, QK_NOPE_DIM:] = kr_ref[...]


def head_keys(c, w_uk, kr, *, tm=2048):
    n, k = c.shape
    tm = _tile(n, tm)
    blocks = _nbytes((tm, k), BF16) + _nbytes((k, QK_NOPE_DIM), BF16) + 3 * _nbytes((tm, LANES), BF16)
    return pl.pallas_call(
        _head_keys_kernel,
        grid=(N_B_HEADS, n // tm),
        in_specs=[pl.BlockSpec((tm, k), lambda h, i: (i, 0)),
                  pl.BlockSpec((k, QK_NOPE_DIM), lambda h, i: (0, h)),
                  pl.BlockSpec((tm, ROPE_PAD), lambda h, i: (i, 0))],
        out_specs=pl.BlockSpec((1, tm, Q_HEAD_PAD), lambda h, i: (h, i, 0)),
        out_shape=jax.ShapeDtypeStruct((N_B_HEADS, n, Q_HEAD_PAD), BF16),
        compiler_params=_params(("parallel", "parallel"), blocks),
        name="head_keys",
    )(c, w_uk, kr)


def _head_values_t_kernel(w_ref, c_ref, o_ref):
    o_ref[0, :V_HEAD_DIM, :] = _dot_nt(w_ref[...], c_ref[...]).astype(o_ref.dtype)
    o_ref[0, V_HEAD_DIM:, :] = jnp.ones((ONES_ROWS, o_ref.shape[2]), o_ref.dtype)


def head_values_t(w_uv_t, c, *, tn=2048):
    n, k = c.shape
    tn = _tile(n, tn)
    rows = V_HEAD_DIM + ONES_ROWS
    blocks = _nbytes((V_HEAD_DIM, k), BF16) + _nbytes((tn, k), BF16) + _nbytes((rows, tn), BF16)
    return pl.pallas_call(
        _head_values_t_kernel,
        grid=(N_B_HEADS, n // tn),
        in_specs=[pl.BlockSpec((V_HEAD_DIM, k), lambda h, j: (h, 0)), pl.BlockSpec((tn, k), lambda h, j: (j, 0))],
        out_specs=pl.BlockSpec((1, rows, tn), lambda h, j: (h, 0, j)),
        out_shape=jax.ShapeDtypeStruct((N_B_HEADS, rows, n), BF16),
        compiler_params=_params(("parallel", "parallel"), blocks),
        name="head_values_t",
    )(w_uv_t, c)


def _mla_prompt_kernel(q_ref, kcat_ref, vt1_ref, o_ref, *, tq, hps):
    qi = pl.program_id(1)
    nv = V_HEAD_DIM
    qs = [q_ref[:, a * Q_HEAD_PAD:(a + 1) * Q_HEAD_PAD] for a in range(hps)]

    def step(kb, carry, masked):
        start = pl.multiple_of(kb * tq, tq)
        scores = [_dot_nt(kcat_ref[a, pl.ds(start, tq), :], qs[a]) for a in range(hps)]
        out = []
        for a in range(hps):
            m, acc = carry[a]
            s = scores[a]
            if masked:
                key = lax.broadcasted_iota(jnp.int32, (tq, tq), 0)
                qry = lax.broadcasted_iota(jnp.int32, (tq, tq), 1)
                s = jnp.where(key <= qry, s, NEG)
            m_new = jnp.maximum(m, jnp.max(s, axis=0, keepdims=True))
            alpha = jnp.exp(m - m_new)
            p = jnp.exp(s - m_new).astype(BF16)
            acc = alpha * acc + _dot(vt1_ref[a, :, pl.ds(start, tq)], p)
            out.append((m_new, acc))
        return tuple(out)

    init = tuple((jnp.full((1, tq), NEG, F32), jnp.zeros((vt1_ref.shape[1], tq), F32)) for _ in range(hps))
    carry = lax.fori_loop(0, qi, lambda kb, c: step(kb, c, False), init)
    carry = step(qi, carry, True)
    for a, (_, acc) in enumerate(carry):
        o_ref[:, a * nv:(a + 1) * nv] = (acc[:nv] / acc[nv:nv + 1]).T.astype(o_ref.dtype)


def mla_prompt(q, kcat, vt1, *, tq=512, heads_per_step=4):
    s_len = q.shape[0]
    tq = _tile(s_len, tq)
    hps = heads_per_step
    vrows = vt1.shape[1]
    blocks = _nbytes((tq, hps * Q_HEAD_PAD), BF16) + _nbytes((tq, hps * V_HEAD_DIM), BF16)
    resident = (hps * _nbytes((s_len, Q_HEAD_PAD), BF16) + hps * _nbytes((vrows, s_len), BF16)) // 2
    return pl.pallas_call(
        functools.partial(_mla_prompt_kernel, tq=tq, hps=hps),
        grid=(N_B_HEADS // hps, s_len // tq),
        in_specs=[pl.BlockSpec((tq, hps * Q_HEAD_PAD), lambda h, i: (i, h)),
                  pl.BlockSpec((hps, s_len, Q_HEAD_PAD), lambda h, i: (h, 0, 0), pipeline_mode=pl.Buffered(1)),
                  pl.BlockSpec((hps, vrows, s_len), lambda h, i: (h, 0, 0), pipeline_mode=pl.Buffered(1))],
        out_specs=pl.BlockSpec((tq, hps * V_HEAD_DIM), lambda h, i: (i, h)),
        out_shape=jax.ShapeDtypeStruct((s_len, N_B_HEADS * V_HEAD_DIM), BF16),
        compiler_params=_params(("parallel", "arbitrary"), blocks + 4 * hps * _nbytes((tq, tq), F32), resident),
        name="mla_prompt",
    )(q, kcat, vt1)


def _row_scores(q, k, bias):
    return jnp.sum(k * q[None], axis=-1, keepdims=True) * ATTN_SCALE + bias


def _dswa_sample_kernel(buf_ref, new_ref, q_ref, bias_a_ref, bias_b_ref, nbuf_ref, o_ref, lse_ref, *, d):
    nsub = buf_ref.shape[1]
    nh = HEADS_PER_GROUP
    t_new = new_ref.shape[1]
    for r_out in range(d):
        sh, r_in = divmod(r_out + t_new, d)
        if nsub - sh > 0:
            nbuf_ref[0, 0:nsub - sh, r_out] = buf_ref[0, sh:nsub, r_in]
        for a in range(max(nsub - sh, 0), nsub):
            nbuf_ref[0, a, r_out] = new_ref[0, a * d + r_out + t_new - nsub * d]
    k_new = new_ref[0, :, 0:nh, :]
    v_new = new_ref[0, :, nh:2 * nh, :]
    for i in range(t_new):
        q = q_ref[0, i]
        k_old = buf_ref[0, :, i % d, 0:nh, :]
        v_old = buf_ref[0, :, i % d, nh:2 * nh, :]
        s_a = _row_scores(q, k_old, bias_a_ref[i])
        s_b = _row_scores(q, k_new, bias_b_ref[i])
        m = jnp.maximum(jnp.max(s_a, axis=0), jnp.max(s_b, axis=0))
        e_a = jnp.exp(s_a - m[None])
        e_b = jnp.exp(s_b - m[None])
        l = jnp.sum(e_a, axis=0) + jnp.sum(e_b, axis=0)
        acc = jnp.sum(e_a * v_old, axis=0) + jnp.sum(e_b * v_new, axis=0)
        o_ref[0, i] = acc / l
        lse_ref[0, i] = m + jnp.log(l)


def dswa_sample(buf, new_kv, q, bias_a, bias_b, d):
    b, l_buf = buf.shape[:2]
    t_new = new_kv.shape[1]
    nsub = l_buf // d
    rows = 2 * HEADS_PER_GROUP
    bufv = buf.reshape(b, nsub, d, rows, HEAD_DIM)
    buf_spec = pl.BlockSpec((1, nsub, d, rows, HEAD_DIM), lambda i: (i, 0, 0, 0, 0))
    tok = lambda r: pl.BlockSpec((1, t_new, r, HEAD_DIM), lambda i: (i, 0, 0, 0))
    full = lambda a: pl.BlockSpec(a.shape, lambda i: (0,) * a.ndim)
    blocks = 2 * _nbytes((l_buf, rows, HEAD_DIM), F32) + _nbytes(bias_a.shape, F32)
    nbuf, o, lse = pl.pallas_call(
        functools.partial(_dswa_sample_kernel, d=d),
        grid=(b,),
        in_specs=[buf_spec, tok(rows), tok(HEADS_PER_GROUP), full(bias_a), full(bias_b)],
        out_specs=[buf_spec, tok(HEADS_PER_GROUP), tok(HEADS_PER_GROUP)],
        out_shape=[jax.ShapeDtypeStruct(bufv.shape, F32),
                   jax.ShapeDtypeStruct((b, t_new, HEADS_PER_GROUP, HEAD_DIM), F32),
                   jax.ShapeDtypeStruct((b, t_new, HEADS_PER_GROUP, HEAD_DIM), F32)],
        compiler_params=_params(("parallel",), blocks),
        name=f"dswa_sample_d{d}",
    )(bufv, new_kv, q, bias_a, bias_b)
    return nbuf.reshape(buf.shape), o, lse


def _dswa_sample_dense_kernel(buf_ref, new_ref, q_ref, bias_ref, nbuf_ref, o_ref, lse_ref):
    nh = HEADS_PER_GROUP
    l_buf, rows = buf_ref.shape[1], buf_ref.shape[2]
    t_new = new_ref.shape[1]
    ncol = (l_buf + t_new) * rows
    data, scores = [], []
    for j in range(q_ref.shape[0]):
        x, fresh = buf_ref[j], new_ref[j]
        nbuf_ref[j, 0:l_buf - t_new] = x[t_new:]
        nbuf_ref[j, l_buf - t_new:] = fresh
        src = jnp.concatenate([x, fresh], axis=0)
        swapped = pltpu.roll(src, nh, axis=1).reshape(ncol, HEAD_DIM).astype(BF16)
        data.append(src.reshape(ncol, HEAD_DIM).astype(BF16))
        scores.append(_dot_nt(q_ref[j].astype(BF16), swapped) * ATTN_SCALE + bias_ref[...])
    for j, s in enumerate(scores):
        m = jnp.max(s, axis=-1, keepdims=True)
        e = jnp.exp(s - m)
        l = jnp.sum(e, axis=-1, keepdims=True)
        o_ref[j] = _dot(e.astype(BF16), data[j]) / l
        lse_ref[j] = jnp.broadcast_to(m + jnp.log(l), o_ref.shape[1:])


def dswa_sample_dense(buf, new_kv, q, bias):
    b, l_buf, rows, _ = buf.shape
    seq_bytes = _nbytes((l_buf, rows, HEAD_DIM), F32)
    nb = max(1, min(8, (4 << 20) // seq_bytes))
    nb = nb if b % nb == 0 else 1
    lead = lambda a: pl.BlockSpec((nb,) + a.shape[1:], lambda i: (i,) + (0,) * (a.ndim - 1))
    blocks = 2 * nb * seq_bytes + _nbytes(bias.shape, F32)
    return pl.pallas_call(
        _dswa_sample_dense_kernel,
        grid=(b // nb,),
        in_specs=[lead(buf), lead(new_kv), lead(q), pl.BlockSpec(bias.shape, lambda i: (0, 0))],
        out_specs=[lead(buf), lead(q), lead(q)],
        out_shape=[jax.ShapeDtypeStruct(buf.shape, F32), jax.ShapeDtypeStruct(q.shape, F32),
                   jax.ShapeDtypeStruct(q.shape, F32)],
        compiler_params=_params(("parallel",), blocks, 2 * nb * seq_bytes),
        name=f"dswa_sample_dense_l{l_buf}",
    )(buf, new_kv, q, bias)


def _mem_attn_sample_kernel(kv_ref, q_ref, o_ref):
    nh = N_MEM_HEADS
    nq = q_ref.shape[1]
    n_mem, rows = kv_ref.shape[2], kv_ref.shape[3]
    ncol = n_mem * rows
    col_row = lax.broadcasted_iota(jnp.int32, (nq, ncol), 1) % rows
    q_head = lax.broadcasted_iota(jnp.int32, (nq, ncol), 0) // (nq // nh)
    own = col_row == q_head + nh
    data, scores = [], []
    for j in range(q_ref.shape[0]):
        x = kv_ref[0, j]
        swapped = pltpu.roll(x, nh, axis=1).reshape(ncol, HEAD_DIM).astype(BF16)
        data.append(x.reshape(ncol, HEAD_DIM).astype(BF16))
        scores.append(_dot_nt(q_ref[j].astype(BF16), swapped) * ATTN_SCALE)
    for j, s in enumerate(scores):
        s = jnp.where(own, s, NEG)
        e = jnp.exp(s - jnp.max(s, axis=-1, keepdims=True))
        o_ref[j] = _dot(e.astype(BF16), data[j]) / jnp.sum(e, axis=-1, keepdims=True)


def mem_attn_sample(cache, layer, q, *, seqs_per_step=8):
    _, b, n_mem, rows, _ = cache.shape
    nb = seqs_per_step if b % seqs_per_step == 0 else 1
    tok = pl.BlockSpec((nb,) + q.shape[1:], lambda i: (i, 0, 0))
    return pl.pallas_call(
        _mem_attn_sample_kernel,
        grid=(b // nb,),
        in_specs=[pl.BlockSpec((1, nb, n_mem, rows, HEAD_DIM), lambda i: (layer, i, 0, 0, 0)), tok],
        out_specs=tok,
        out_shape=jax.ShapeDtypeStruct(q.shape, F32),
        compiler_params=_params(("parallel",), nb * _nbytes((n_mem, rows, HEAD_DIM), F32),
                                2 * nb * _nbytes((n_mem, rows, HEAD_DIM), F32)),
        name="mem_attn_sample",
    )(cache, q)


def _mla_sample_kernel(pt_ref, q_ref, new_ref, *rest, n_pages, n_seq, t_new):
    del pt_ref
    page_refs = rest[:n_pages * n_seq]
    o_ref, m_ref, l_ref, acc_ref = rest[n_pages * n_seq:]
    c = pl.program_id(1)
    nq = q_ref.shape[1]

    @pl.when(c == 0)
    def _():
        m_ref[...] = jnp.full(m_ref.shape, NEG, F32)
        l_ref[...] = jnp.zeros(l_ref.shape, F32)
        acc_ref[...] = jnp.zeros(acc_ref.shape, F32)

    def update(j, s_parts, pv_fn):
        m_prev = m_ref[j]
        m_cur = jnp.max(functools.reduce(jnp.maximum, s_parts), axis=-1, keepdims=True)
        m_new = jnp.maximum(m_prev, m_cur)
        alpha = jnp.exp(m_prev - m_new)
        p = [jnp.exp(s - m_new) for s in s_parts]
        l_ref[j] = alpha * l_ref[j] + jnp.sum(functools.reduce(jnp.add, p), axis=-1, keepdims=True)
        m_ref[j] = m_new
        acc_ref[j] = alpha * acc_ref[j] + pv_fn([x.astype(BF16) for x in p])

    pages = [[r[0].astype(BF16) for r in page_refs[j * n_pages:(j + 1) * n_pages]] for j in range(n_seq)]
    scores = [[_dot(q_ref[j], kt) for kt in pages[j]] for j in range(n_seq)]
    for j in range(n_seq):
        update(j, scores[j], lambda p, kts=pages[j]: functools.reduce(
            jnp.add, [_dot_nt(pi, kt[:KV_LORA_RANK]) for pi, kt in zip(p, kts)]))

    @pl.when(c == pl.num_programs(1) - 1)
    def _():
        tok = lax.broadcasted_iota(jnp.int32, (nq, t_new), 0) // (nq // t_new)
        key = lax.broadcasted_iota(jnp.int32, (nq, t_new), 1)
        for j in range(n_seq):
            nr = new_ref[j].astype(BF16)
            s = jnp.where(key <= tok, _dot_nt(q_ref[j], nr), NEG)
            update(j, [s], lambda p, nr=nr: _dot(p[0], nr[:, :KV_LORA_RANK]))
            o_ref[j] = acc_ref[j] / l_ref[j]


def mla_sample(q, new_rows, cache_t, page_table):
    b, nq, row = q.shape
    t_new = new_rows.shape[1]
    n_seq_pages = page_table.shape[1]
    pps = PAGES_PER_STEP if n_seq_pages % PAGES_PER_STEP == 0 else 1
    nb = SEQS_PER_STEP if b % SEQS_PER_STEP == 0 else 1

    def page_spec(j, p):
        return pl.BlockSpec((1, row, PAGE_SIZE), lambda i, c, pt: (pt[i * nb + j, c * pps + p], 0, 0))

    blocks = (nb * pps * _nbytes((row, PAGE_SIZE), F32) + nb * _nbytes((nq, row), BF16)
              + nb * _nbytes((nq, KV_LORA_RANK), F32))
    return pl.pallas_call(
        functools.partial(_mla_sample_kernel, n_pages=pps, n_seq=nb, t_new=t_new),
        grid_spec=pltpu.PrefetchScalarGridSpec(
            num_scalar_prefetch=1,
            grid=(b // nb, n_seq_pages // pps),
            in_specs=[pl.BlockSpec((nb, nq, row), lambda i, c, pt: (i, 0, 0)),
                      pl.BlockSpec((nb, t_new, row), lambda i, c, pt: (i, 0, 0))]
                     + [page_spec(j, p) for j in range(nb) for p in range(pps)],
            out_specs=pl.BlockSpec((nb, nq, KV_LORA_RANK), lambda i, c, pt: (i, 0, 0)),
            scratch_shapes=[pltpu.VMEM((nb, nq, 1), F32), pltpu.VMEM((nb, nq, 1), F32),
                            pltpu.VMEM((nb, nq, KV_LORA_RANK), F32)]),
        out_shape=jax.ShapeDtypeStruct((b, nq, KV_LORA_RANK), F32),
        compiler_params=_params(("parallel", "arbitrary"), blocks,
                                nb * pps * _nbytes((row, PAGE_SIZE), BF16)),
        name="mla_sample",
    )(page_table, q, new_rows, *([cache_t] * (nb * pps)))


def _t5_bucket(dist):
    max_exact = N_BUCKETS // 2
    dd = jnp.maximum(dist, 1).astype(F32)
    large = max_exact + (jnp.log(dd / max_exact) / math.log(MAX_DISTANCE / max_exact)
                         * (N_BUCKETS - max_exact)).astype(jnp.int32)
    large = jnp.minimum(large, N_BUCKETS - 1)
    return jnp.where(dist < max_exact, dist, large)


def _group_bias(t5_bias, g):
    w, d = DSWA_GROUPS[g]
    offs = jnp.arange(w // d + 1, dtype=jnp.int32) * d
    return t5_bias[_t5_bucket(offs), g * HEADS_PER_GROUP:(g + 1) * HEADS_PER_GROUP].astype(F32)


def _banded(table, j):
    jmax = table.shape[0] - 1
    vals = table[jnp.clip(j, 0, jmax)]
    return jnp.where(((j >= 0) & (j <= jmax))[..., None], vals, NEG)


def _prompt_bias(table):
    blk, nh = table.shape[0] - 1, table.shape[1]
    n = 3 * blk - 1
    pad = jnp.full((blk - 1, nh), NEG, F32)
    u = jnp.concatenate([pad, table, pad], axis=0)
    skew = jnp.tile(u, (blk + 1, 1))[:blk * (n + 1)].reshape(blk, n + 1, nh)
    return jnp.moveaxis(skew[:, :2 * blk][:, ::-1], -1, 0)


def _sample_bias(table, d, t_new):
    blk = table.shape[0] - 1
    i = jnp.arange(t_new)
    j_a = blk - jnp.arange(blk)[None, :] + (i // d)[:, None]
    diff = i[:, None] - i[None, :]
    j_b = jnp.where((diff >= 0) & (diff % d == 0), diff // d, -1)
    lanes = lambda x: jnp.broadcast_to(x[..., None], x.shape + (HEAD_DIM,))
    return lanes(_banded(table, j_a)), lanes(_banded(table, j_b))


def _sample_bias_dense(table, d, t_new):
    blk, nh = table.shape[0] - 1, table.shape[1]
    w = blk * d
    stuffed = jnp.concatenate([table[:, None, :], jnp.full((blk + 1, d - 1, nh), NEG, F32)], axis=1)
    stuffed = stuffed.reshape((blk + 1) * d, nh)[:w + 1]
    pad = jnp.full((t_new - 1, nh), NEG, F32)
    rev = jnp.concatenate([pad, stuffed, pad], axis=0)[::-1]
    band = jnp.stack([rev[t_new - 1 - i:t_new - 1 - i + w + t_new] for i in range(t_new)])
    band = jnp.transpose(band, (2, 0, 1))[..., None]
    head = jnp.arange(nh)[:, None, None, None]
    sub = jnp.arange(2 * nh)[None, None, None, :]
    return jnp.where(sub == head + nh, band, NEG).reshape(nh * t_new, (w + t_new) * 2 * nh)


def _rope_tables(pos):
    half = QK_ROPE_DIM // 2
    inv_freq = ROPE_THETA ** (-jnp.arange(half, dtype=F32) / half)
    ang = pos.astype(F32)[:, None] * inv_freq[None, :]
    cos, sin = jnp.cos(ang), jnp.sin(ang)
    zero = jnp.zeros_like(cos)
    cc = jnp.concatenate([cos, cos, zero, zero], axis=-1)
    shi = jnp.concatenate([zero, sin, zero, zero], axis=-1)
    slo = jnp.concatenate([-sin, zero, zero, zero], axis=-1)
    return cc, shi, slo


def _pad_last(w, to):
    return jnp.pad(w, [(0, 0)] * (w.ndim - 1) + [(0, to - w.shape[-1])])


def _trunk(x, rope_tables, dswa_fn, mem_fn, mla_fn, p):
    z0 = norm_matmul(x, p['g_attn_pre'][0], p['w_a_in'])
    o_tok = dswa_fn(z0)
    o_mem = mem_fn(0, z0, A_QKV_WIDTH // MEM_WIDTH)
    x = out_proj(o_tok, o_mem, p['w_a_out'][:GROUP_WIDTH], p['w_a_out'][GROUP_WIDTH:], p['g_attn_post'][0], x)
    x = mlp(x, p['g_mlp_pre'][0], p['w_mlp_up'][0], p['w_mlp_down'][0], p['g_mlp_post'][0])
    raw = norm_matmul(x, p['g_kv_in'], p['w_kv_down'])
    rows, c_bf, kr_bf = kv_finish(raw, p['g_kv_latent'], rope_tables)
    z = norm_matmul(x, p['g_attn_pre'][1], p['w_b_in'])
    q = norm_matmul(z, p['g_q_latent'], p['w_q_up'], k_width=Q_LORA_RANK, rope_tables=rope_tables,
                    scale=MLA_SCALE, out_dtype=BF16, tn=2 * Q_HEAD_PAD)
    o_tok = mla_fn(q, rows, c_bf, kr_bf)
    o_mem = mem_fn(1, z, Q_LORA_RANK // MEM_WIDTH)
    nv = N_B_HEADS * V_HEAD_DIM
    x = out_proj(o_tok, o_mem, p['w_b_out'][:nv], p['w_b_out'][nv:], p['g_attn_post'][1], x)
    x = mlp(x, p['g_mlp_pre'][1], p['w_mlp_up'][1], p['w_mlp_down'][1], p['g_mlp_post'][1])
    return x, rows[:, :MLA_ROW], z0


def kernel(x_prompt, x_sample, mem_prompt, cache_swa_kv_w128, cache_swa_kv_w512, cache_swa_kv_w2048, cache_mla_kv, cache_mem_kv, page_table, t5_bias, g_attn_pre, g_attn_post, g_mlp_pre, g_mlp_post, g_mem, w_mem_kv, w_mlp_up, w_mlp_down, w_a_in, w_a_out, g_kv_in, w_kv_down, g_kv_latent, w_kv_up, w_b_in, g_q_latent, w_q_up, w_b_out):
    depth = g_attn_pre.shape[0]
    assert depth == 2 and w_a_in.shape[0] == 1 and w_b_in.shape[0] == 1
    bp, s_len, d_model = x_prompt.shape
    assert bp == 1
    bs, t_new, _ = x_sample.shape
    past = page_table.shape[1] * PAGE_SIZE
    bufs = (cache_swa_kv_w128, cache_swa_kv_w512, cache_swa_kv_w2048)
    for buf, (w, d) in zip(bufs, DSWA_GROUPS):
        assert buf.shape[2] == w and past >= w and s_len % (d * (w // d)) == 0 and s_len >= w

    w_q_up_pad = _pad_last(w_q_up[0].reshape(Q_LORA_RANK, N_B_HEADS, QK_NOPE_DIM + QK_ROPE_DIM),
                           Q_HEAD_PAD).reshape(Q_LORA_RANK, N_B_HEADS * Q_HEAD_PAD)
    p = {
        'g_attn_pre': g_attn_pre, 'g_attn_post': g_attn_post, 'g_mlp_pre': g_mlp_pre, 'g_mlp_post': g_mlp_post,
        'g_kv_in': g_kv_in, 'g_kv_latent': g_kv_latent, 'g_q_latent': g_q_latent[0],
        'w_a_in': w_a_in[0].astype(BF16), 'w_a_out': w_a_out[0].astype(BF16),
        'w_b_in': w_b_in[0].astype(BF16), 'w_b_out': w_b_out[0].astype(BF16),
        'w_mlp_up': w_mlp_up.astype(BF16), 'w_mlp_down': w_mlp_down.astype(BF16),
        'w_kv_down': _pad_last(w_kv_down, MLA_ROW_PAD).astype(BF16),
        'w_q_up': w_q_up_pad.astype(BF16),
    }
    w_uk = w_kv_up[..., :QK_NOPE_DIM]
    w_uv = w_kv_up[..., QK_NOPE_DIM:]
    w_uk_cat = w_uk.reshape(KV_LORA_RANK, -1).astype(BF16)
    w_uv_cat_t = w_uv.reshape(KV_LORA_RANK, -1).T.astype(BF16)
    w_uk_t = jnp.transpose(w_uk, (1, 2, 0)).astype(BF16)
    w_uv_h = jnp.transpose(w_uv, (1, 0, 2)).astype(BF16)
    tables = [_group_bias(t5_bias, g) for g in range(N_GROUPS)]

    mem_kv_prompt = [norm_matmul(mem_prompt[0], g_mem[l], w_mem_kv[l].astype(BF16)) for l in range(depth)]

    def dswa_p(z):
        parts = [dswa_prompt(z, g, _prompt_bias(tables[g])) for g in range(N_GROUPS)]
        return combine_groups([o for o, _ in parts], [l for _, l in parts])

    def mla_p(q, rows, c_bf, kr_bf):
        return mla_prompt(q, head_keys(c_bf, w_uk_cat, kr_bf), head_values_t(w_uv_cat_t, c_bf))

    y_p, rows_p, z_p = _trunk(x_prompt[0], _rope_tables(jnp.arange(s_len, dtype=jnp.int32)), dswa_p,
                              lambda l, z, cb: mem_attn_prompt(z, cb, mem_kv_prompt[l]), mla_p, p)
    swa_p = []
    for g, (w, _) in enumerate(DSWA_GROUPS):
        k_g = z_p[s_len - w:, (N_GROUPS + g) * GROUP_WIDTH:(N_GROUPS + g + 1) * GROUP_WIDTH]
        v_g = z_p[s_len - w:, (2 * N_GROUPS + g) * GROUP_WIDTH:(2 * N_GROUPS + g + 1) * GROUP_WIDTH]
        kv_g = jnp.stack([k_g.reshape(w, HEADS_PER_GROUP, HEAD_DIM), v_g.reshape(w, HEADS_PER_GROUP, HEAD_DIM)], axis=1)
        swa_p.append(kv_g[None, None])
    mem_kv_out = jnp.stack(mem_kv_prompt).reshape(depth, 1, mem_prompt.shape[1], 2, N_MEM_HEADS, HEAD_DIM)

    ns = bs * t_new
    pos_s = past + (jnp.arange(ns, dtype=jnp.int32) % t_new)
    swa_s = []

    def heads(a, nh):
        return a.reshape(bs, t_new, nh, HEAD_DIM)

    def dswa_s(z):
        outs, lses = [], []
        for g, (w, d) in enumerate(DSWA_GROUPS):
            sec = lambda c: z[:, (c * N_GROUPS + g) * GROUP_WIDTH:(c * N_GROUPS + g + 1) * GROUP_WIDTH]
            new_kv = jnp.concatenate([heads(sec(1), HEADS_PER_GROUP), heads(sec(2), HEADS_PER_GROUP)], axis=2)
            buf = bufs[g][0].reshape(bs, w, 2 * HEADS_PER_GROUP, HEAD_DIM)
            q = heads(sec(0), HEADS_PER_GROUP)
            if w <= DENSE_SAMPLE_MAX_ROWS:
                q = jnp.swapaxes(q, 1, 2).reshape(bs, HEADS_PER_GROUP * t_new, HEAD_DIM)
                nbuf, o, lse = dswa_sample_dense(buf, new_kv, q, _sample_bias_dense(tables[g], d, t_new))
                o, lse = (jnp.swapaxes(a.reshape(bs, HEADS_PER_GROUP, t_new, HEAD_DIM), 1, 2) for a in (o, lse))
            else:
                bias_a, bias_b = _sample_bias(tables[g], d, t_new)
                nbuf, o, lse = dswa_sample(buf, new_kv, q, bias_a, bias_b, d)
            swa_s.append(nbuf.reshape(bufs[g].shape))
            outs.append(o.reshape(ns, GROUP_WIDTH))
            lses.append(lse.reshape(ns, GROUP_WIDTH))
        return combine_groups(outs, lses)

    cache_mem = cache_mem_kv.reshape(depth, bs, cache_mem_kv.shape[2], 2 * N_MEM_HEADS, HEAD_DIM)

    def mem_s(l, z, cb):
        q = heads(z[:, cb * MEM_WIDTH:(cb + 1) * MEM_WIDTH], N_MEM_HEADS)
        q = jnp.swapaxes(q, 1, 2).reshape(bs, N_MEM_HEADS * t_new, HEAD_DIM)
        o = mem_attn_sample(cache_mem, l, q).reshape(bs, N_MEM_HEADS, t_new, HEAD_DIM)
        return jnp.swapaxes(o, 1, 2).reshape(ns, MEM_WIDTH)

    cache_mla_t = jnp.swapaxes(cache_mla_kv, 1, 2)

    def mla_s(q, rows, c_bf, kr_bf):
        q_lat = head_matmul(q, w_uk_t, x_block_stride=Q_HEAD_PAD // QK_NOPE_DIM, out_dtype=BF16)
        q3 = q.reshape(ns, N_B_HEADS, Q_HEAD_PAD)
        q_full = jnp.concatenate([q_lat.reshape(ns, N_B_HEADS, KV_LORA_RANK),
                                  q3[:, :, QK_NOPE_DIM:QK_NOPE_DIM + QK_ROPE_DIM]], axis=-1)
        q_full = q_full.reshape(bs, t_new * N_B_HEADS, MLA_ROW)
        new_rows = rows[:, :MLA_ROW].reshape(bs, t_new, MLA_ROW)
        o_lat = mla_sample(q_full, new_rows, cache_mla_t, page_table)
        return head_matmul(o_lat.reshape(ns, N_B_HEADS * KV_LORA_RANK), w_uv_h, out_dtype=BF16)

    y_s, rows_s, _ = _trunk(x_sample.reshape(ns, d_model), _rope_tables(pos_s), dswa_s, mem_s, mla_s, p)

    return (y_p[None], y_s.reshape(bs, t_new, d_model), swa_p[0], swa_p[1], swa_p[2],
            rows_p[None], mem_kv_out, swa_s[0], swa_s[1], swa_s[2], rows_s.reshape(bs, t_new, MLA_ROW))
```

```python
import functools
import math

import jax
import jax.numpy as jnp
from jax import lax
from jax.experimental import pallas as pl
from jax.experimental.pallas import tpu as pltpu

F32 = jnp.float32
BF16 = jnp.bfloat16

HEAD_DIM = 128
DSWA_GROUPS = ((128, 1), (512, 4), (2048, 16))
N_GROUPS = len(DSWA_GROUPS)
HEADS_PER_GROUP = 4
GROUP_WIDTH = HEADS_PER_GROUP * HEAD_DIM
N_A_HEADS = N_GROUPS * HEADS_PER_GROUP
A_QKV_WIDTH = 3 * N_A_HEADS * HEAD_DIM
N_MEM_HEADS = 4
MEM_WIDTH = N_MEM_HEADS * HEAD_DIM
N_B_HEADS = 12
Q_LORA_RANK = 1536
KV_LORA_RANK = 512
QK_NOPE_DIM = 128
QK_ROPE_DIM = 64
V_HEAD_DIM = 128
MLA_ROW = KV_LORA_RANK + QK_ROPE_DIM
MLA_SCALE = (QK_NOPE_DIM + QK_ROPE_DIM) ** -0.5
ATTN_SCALE = HEAD_DIM ** -0.5
ROPE_THETA = 10000.0
N_BUCKETS = 32
MAX_DISTANCE = 2048
EPS = 1e-6
PAGE_SIZE = 128

LANES = 128
ROPE_PAD = LANES
Q_HEAD_PAD = QK_NOPE_DIM + ROPE_PAD
MLA_ROW_PAD = KV_LORA_RANK + ROPE_PAD
NEG = -1e30
VMEM_BYTES_V7X = 64 * 1024 * 1024
VMEM_CAP = VMEM_BYTES_V7X - 8 * 1024 * 1024
PAGES_PER_STEP = 16
SEQS_PER_STEP = 4
DENSE_SAMPLE_MAX_ROWS = 512


def _tile(n, pref):
    return pref if n % pref == 0 else n


def _nbytes(shape, dtype):
    return math.prod(shape) * jnp.dtype(dtype).itemsize


def _params(semantics, pipelined_bytes, resident_bytes=0):
    est = 2 * pipelined_bytes + 2 * resident_bytes + (4 << 20)
    return pltpu.CompilerParams(dimension_semantics=semantics,
                                vmem_limit_bytes=int(min(max(est, 16 << 20), VMEM_CAP)))


def _rms(x, g):
    return x * lax.rsqrt(jnp.mean(x * x, axis=-1, keepdims=True) + EPS) * g


def _rope_pad(r, cc, shi, slo):
    half = QK_ROPE_DIM // 2
    return r * cc + pltpu.roll(r, half, axis=1) * shi + pltpu.roll(r, LANES - half, axis=1) * slo


def _dot(a, b):
    return jnp.dot(a, b, preferred_element_type=F32)


def _dot_nt(a, b):
    return lax.dot_general(a, b, (((1,), (1,)), ((), ())), preferred_element_type=F32)


def _norm_matmul_kernel(*refs, norm, rope, scale):
    if rope:
        x_ref, g_ref, w_ref, cc_ref, shi_ref, slo_ref, o_ref, xn_ref = refs
    else:
        x_ref, g_ref, w_ref, o_ref, xn_ref = refs

    @pl.when(pl.program_id(1) == 0)
    def _():
        x = x_ref[...].astype(F32)
        if norm:
            x = _rms(x, g_ref[...])
        xn_ref[...] = x.astype(BF16)

    acc = _dot(xn_ref[...], w_ref[...])
    if scale != 1.0:
        acc = acc * scale
    if rope:
        cc, shi, slo = cc_ref[...], shi_ref[...], slo_ref[...]
        for h in range(acc.shape[1] // Q_HEAD_PAD):
            lo = h * Q_HEAD_PAD
            o_ref[:, lo:lo + QK_NOPE_DIM] = acc[:, lo:lo + QK_NOPE_DIM].astype(o_ref.dtype)
            r = _rope_pad(acc[:, lo + QK_NOPE_DIM:lo + Q_HEAD_PAD], cc, shi, slo)
            o_ref[:, lo + QK_NOPE_DIM:lo + Q_HEAD_PAD] = r.astype(o_ref.dtype)
    else:
        o_ref[...] = acc.astype(o_ref.dtype)


def norm_matmul(x, g, w, *, k_width=None, norm=True, rope_tables=None, scale=1.0,
                out_dtype=F32, tm=1024, tn=512):
    n = x.shape[0]
    k = k_width or x.shape[1]
    nout = w.shape[1]
    tm = _tile(n, tm)
    tn = _tile(nout, tn)
    rope = rope_tables is not None
    in_specs = [pl.BlockSpec((tm, k), lambda i, j: (i, 0)),
                pl.BlockSpec((1, k), lambda i, j: (0, 0)),
                pl.BlockSpec((k, tn), lambda i, j: (0, j))]
    args = [x, g.reshape(1, k).astype(F32), w]
    if rope:
        in_specs += [pl.BlockSpec((tm, LANES), lambda i, j: (i, 0))] * 3
        args += list(rope_tables)
    blocks = (_nbytes((tm, k), x.dtype) + _nbytes((k, tn), w.dtype) + _nbytes((tm, tn), out_dtype)
              + 3 * _nbytes((tm, LANES), F32))
    return pl.pallas_call(
        functools.partial(_norm_matmul_kernel, norm=norm, rope=rope, scale=scale),
        grid=(n // tm, nout // tn),
        in_specs=in_specs,
        out_specs=pl.BlockSpec((tm, tn), lambda i, j: (i, j)),
        out_shape=jax.ShapeDtypeStruct((n, nout), out_dtype),
        scratch_shapes=[pltpu.VMEM((tm, k), BF16)],
        compiler_params=_params(("parallel", "arbitrary"), blocks,
                                _nbytes((tm, k), F32) + _nbytes((tm, tn), F32)),
        name="norm_matmul",
    )(*args)


def _kv_finish_kernel(raw_ref, g_ref, cc_ref, shi_ref, slo_ref, rows_ref, c_ref, kr_ref):
    c = _rms(raw_ref[:, :KV_LORA_RANK], g_ref[...])
    kr = _rope_pad(raw_ref[:, KV_LORA_RANK:], cc_ref[...], shi_ref[...], slo_ref[...])
    rows_ref[:, :KV_LORA_RANK] = c
    rows_ref[:, KV_LORA_RANK:] = kr
    c_ref[...] = c.astype(BF16)
    kr_ref[...] = kr.astype(BF16)


def kv_finish(raw, g_latent, rope_tables, *, tm=512):
    n = raw.shape[0]
    tm = _tile(n, tm)
    row = lambda w: pl.BlockSpec((tm, w), lambda i: (i, 0))
    blocks = 2 * _nbytes((tm, MLA_ROW_PAD), F32) + 4 * _nbytes((tm, LANES), F32) + _nbytes((tm, KV_LORA_RANK), BF16)
    return pl.pallas_call(
        _kv_finish_kernel,
        grid=(n // tm,),
        in_specs=[row(MLA_ROW_PAD), pl.BlockSpec((1, KV_LORA_RANK), lambda i: (0, 0)),
                  row(LANES), row(LANES), row(LANES)],
        out_specs=[row(MLA_ROW_PAD), row(KV_LORA_RANK), row(LANES)],
        out_shape=[jax.ShapeDtypeStruct((n, MLA_ROW_PAD), F32),
                   jax.ShapeDtypeStruct((n, KV_LORA_RANK), BF16),
                   jax.ShapeDtypeStruct((n, LANES), BF16)],
        compiler_params=_params(("parallel",), blocks),
        name="kv_finish",
    )(raw, g_latent.reshape(1, -1).astype(F32), *rope_tables)


def _out_proj_kernel(a1_ref, a2_ref, w1_ref, w2_ref, g_ref, x_ref, o_ref):
    o = _dot(a1_ref[...].astype(BF16), w1_ref[...]) + _dot(a2_ref[...].astype(BF16), w2_ref[...])
    o_ref[...] = x_ref[...] + _rms(o, g_ref[...])


def out_proj(a1, a2, w1, w2, g, x, *, tm=512):
    n, d = x.shape
    k1, k2 = a1.shape[1], a2.shape[1]
    tm = _tile(n, tm)
    blocks = (_nbytes((tm, k1), a1.dtype) + _nbytes((tm, k2), a2.dtype) + _nbytes((k1 + k2, d), BF16)
              + 2 * _nbytes((tm, d), F32))
    return pl.pallas_call(
        _out_proj_kernel,
        grid=(n // tm,),
        in_specs=[pl.BlockSpec((tm, k1), lambda i: (i, 0)), pl.BlockSpec((tm, k2), lambda i: (i, 0)),
                  pl.BlockSpec((k1, d), lambda i: (0, 0)), pl.BlockSpec((k2, d), lambda i: (0, 0)),
                  pl.BlockSpec((1, d), lambda i: (0, 0)), pl.BlockSpec((tm, d), lambda i: (i, 0))],
        out_specs=pl.BlockSpec((tm, d), lambda i: (i, 0)),
        out_shape=jax.ShapeDtypeStruct((n, d), F32),
        compiler_params=_params(("parallel",), blocks, _nbytes((tm, d), F32)),
        name="out_proj",
    )(a1, a2, w1, w2, g.reshape(1, d).astype(F32), x)


def _mlp_kernel(x_ref, gpre_ref, wup_ref, wdn_ref, gpost_ref, o_ref, xn_ref, acc_ref):
    j = pl.program_id(1)

    @pl.when(j == 0)
    def _():
        xn_ref[...] = _rms(x_ref[...], gpre_ref[...]).astype(BF16)

    h = _dot(xn_ref[...], wup_ref[...])
    u = jnp.square(jnp.maximum(h, 0.0)).astype(BF16)
    part = _dot(u, wdn_ref[...])

    @pl.when(j == 0)
    def _():
        acc_ref[...] = part

    @pl.when(j > 0)
    def _():
        acc_ref[...] += part

    @pl.when(j == pl.num_programs(1) - 1)
    def _():
        o_ref[...] = x_ref[...] + _rms(acc_ref[...], gpost_ref[...])


def mlp(x, g_pre, w_up, w_down, g_post, *, tm=512, tf=1024):
    n, d = x.shape
    f = w_up.shape[1]
    tm = _tile(n, tm)
    tf = _tile(f, tf)
    blocks = 2 * _nbytes((tm, d), F32) + 2 * _nbytes((d, tf), BF16)
    resident = _nbytes((tm, d), BF16) + _nbytes((tm, d), F32) + _nbytes((tm, tf), F32)
    return pl.pallas_call(
        _mlp_kernel,
        grid=(n // tm, f // tf),
        in_specs=[pl.BlockSpec((tm, d), lambda i, j: (i, 0)), pl.BlockSpec((1, d), lambda i, j: (0, 0)),
                  pl.BlockSpec((d, tf), lambda i, j: (0, j)), pl.BlockSpec((tf, d), lambda i, j: (j, 0)),
                  pl.BlockSpec((1, d), lambda i, j: (0, 0))],
        out_specs=pl.BlockSpec((tm, d), lambda i, j: (i, 0)),
        out_shape=jax.ShapeDtypeStruct((n, d), F32),
        scratch_shapes=[pltpu.VMEM((tm, d), BF16), pltpu.VMEM((tm, d), F32)],
        compiler_params=_params(("parallel", "arbitrary"), blocks, resident),
        name="mlp",
    )(x, g_pre.reshape(1, d).astype(F32), w_up, w_down, g_post.reshape(1, d).astype(F32))


def _head_matmul_kernel(x_ref, w_ref, o_ref):
    o_ref[...] = _dot(x_ref[...].astype(BF16), w_ref[0]).astype(o_ref.dtype)


def head_matmul(x, w, *, x_block_stride=1, out_dtype=F32):
    n = x.shape[0]
    nh, kh, nn = w.shape
    blocks = _nbytes((n, kh), x.dtype) + _nbytes((kh, nn), w.dtype) + _nbytes((n, nn), out_dtype)
    return pl.pallas_call(
        _head_matmul_kernel,
        grid=(nh,),
        in_specs=[pl.BlockSpec((n, kh), lambda h: (0, h * x_block_stride)),
                  pl.BlockSpec((1, kh, nn), lambda h: (h, 0, 0))],
        out_specs=pl.BlockSpec((n, nn), lambda h: (0, h)),
        out_shape=jax.ShapeDtypeStruct((n, nh * nn), out_dtype),
        compiler_params=_params(("parallel",), blocks),
        name="head_matmul",
    )(x, w)


def _out_proj_groups_kernel(o0, o1, o2, l0, l1, l2, a2_ref, w1_ref, w2_ref, g_ref, x_ref, o_ref):
    m = jnp.maximum(jnp.maximum(l0[...], l1[...]), l2[...])
    e0, e1, e2 = jnp.exp(l0[...] - m), jnp.exp(l1[...] - m), jnp.exp(l2[...] - m)
    a1 = (e0 * o0[...] + e1 * o1[...] + e2 * o2[...]) / (e0 + e1 + e2)
    o = _dot(a1.astype(BF16), w1_ref[...]) + _dot(a2_ref[...].astype(BF16), w2_ref[...])
    o_ref[...] = x_ref[...] + _rms(o, g_ref[...])


def out_proj_groups(outs, lses, a2, w1, w2, g, x, *, tm=512):
    n, d = x.shape
    k1, k2 = outs[0].shape[1], a2.shape[1]
    tm = _tile(n, tm)
    part = pl.BlockSpec((tm, k1), lambda i: (i, 0))
    blocks = 6 * _nbytes((tm, k1), F32) + _nbytes((tm, k2), a2.dtype) + _nbytes((k1 + k2, d), BF16) \
        + 2 * _nbytes((tm, d), F32)
    return pl.pallas_call(
        _out_proj_groups_kernel,
        grid=(n // tm,),
        in_specs=[part] * 6 + [pl.BlockSpec((tm, k2), lambda i: (i, 0)),
                               pl.BlockSpec((k1, d), lambda i: (0, 0)), pl.BlockSpec((k2, d), lambda i: (0, 0)),
                               pl.BlockSpec((1, d), lambda i: (0, 0)), pl.BlockSpec((tm, d), lambda i: (i, 0))],
        out_specs=pl.BlockSpec((tm, d), lambda i: (i, 0)),
        out_shape=jax.ShapeDtypeStruct((n, d), F32),
        compiler_params=_params(("parallel",), blocks, _nbytes((tm, d), F32)),
        name="out_proj_groups",
    )(*outs, *lses, a2, w1, w2, g.reshape(1, d).astype(F32), x)


def _dswa_prompt_kernel(q_ref, kp_ref, kc_ref, vp_ref, vc_ref, bias_ref, o_ref, lse_ref):
    blk = q_ref.shape[0]
    col = lax.broadcasted_iota(jnp.int32, (blk, 2 * blk), 1)
    prev_invalid = col < jnp.where(pl.program_id(1) == 0, blk, 0)
    for h in range(HEADS_PER_GROUP):
        sl = slice(h * HEAD_DIM, (h + 1) * HEAD_DIM)
        q = q_ref[:, sl].astype(BF16)
        k = jnp.concatenate([kp_ref[:, sl], kc_ref[:, sl]], axis=0).astype(BF16)
        v = jnp.concatenate([vp_ref[:, sl], vc_ref[:, sl]], axis=0).astype(BF16)
        s = _dot_nt(q, k) * ATTN_SCALE + bias_ref[h]
        s = jnp.where(prev_invalid, NEG, s)
        m = jnp.max(s, axis=-1, keepdims=True)
        e = jnp.exp(s - m)
        l = jnp.sum(e, axis=-1, keepdims=True)
        acc = _dot(e.astype(BF16), v)
        o_ref[:, sl] = acc / l
        lse_ref[:, sl] = jnp.broadcast_to(m + jnp.log(l), (blk, HEAD_DIM))


def dswa_prompt(z, g, bias):
    s_len, zw = z.shape
    w, d = DSWA_GROUPS[g]
    blk = w // d
    sub = s_len // d
    nblk = sub // blk
    q_col, k_col, v_col = g, N_GROUPS + g, 2 * N_GROUPS + g
    if d == 1:
        zv, per_res = z, zw // GROUP_WIDTH
    else:
        zg = jnp.concatenate([z[:, c * GROUP_WIDTH:(c + 1) * GROUP_WIDTH] for c in (q_col, k_col, v_col)], axis=1)
        zv, per_res = zg.reshape(sub, d * 3 * GROUP_WIDTH), 3
        q_col, k_col, v_col = 0, 1, 2
    cur = lambda c: pl.BlockSpec((blk, GROUP_WIDTH), lambda r, u: (u, r * per_res + c))
    prev = lambda c: pl.BlockSpec((blk, GROUP_WIDTH), lambda r, u: (jnp.maximum(u - 1, 0), r * per_res + c))
    out_spec = pl.BlockSpec((blk, GROUP_WIDTH), lambda r, u: (u, r))
    blocks = 7 * _nbytes((blk, GROUP_WIDTH), F32) + _nbytes(bias.shape, F32)
    o, lse = pl.pallas_call(
        _dswa_prompt_kernel,
        grid=(d, nblk),
        in_specs=[cur(q_col), prev(k_col), cur(k_col), prev(v_col), cur(v_col),
                  pl.BlockSpec(bias.shape, lambda r, u: (0, 0, 0))],
        out_specs=[out_spec, out_spec],
        out_shape=[jax.ShapeDtypeStruct((sub, d * GROUP_WIDTH), F32)] * 2,
        compiler_params=_params(("parallel", "arbitrary"), blocks),
        name=f"dswa_prompt_g{g}",
    )(zv, zv, zv, zv, zv, bias)
    return o.reshape(s_len, GROUP_WIDTH), lse.reshape(s_len, GROUP_WIDTH)


def _mem_attn_prompt_kernel(q_ref, kv_ref, o_ref):
    for h in range(N_MEM_HEADS):
        sl = slice(h * HEAD_DIM, (h + 1) * HEAD_DIM)
        q = q_ref[:, sl].astype(BF16)
        k = kv_ref[:, sl].astype(BF16)
        v = kv_ref[:, MEM_WIDTH + h * HEAD_DIM:MEM_WIDTH + (h + 1) * HEAD_DIM].astype(BF16)
        s = _dot_nt(q, k) * ATTN_SCALE
        m = jnp.max(s, axis=-1, keepdims=True)
        e = jnp.exp(s - m)
        l = jnp.sum(e, axis=-1, keepdims=True)
        o_ref[:, sl] = _dot(e.astype(BF16), v) / l


def mem_attn_prompt(z, col_block, mem_kv, *, tm=512):
    n = z.shape[0]
    tm = _tile(n, tm)
    blocks = 2 * _nbytes((tm, MEM_WIDTH), F32) + _nbytes(mem_kv.shape, F32)
    return pl.pallas_call(
        _mem_attn_prompt_kernel,
        grid=(n // tm,),
        in_specs=[pl.BlockSpec((tm, MEM_WIDTH), lambda i: (i, col_block)),
                  pl.BlockSpec(mem_kv.shape, lambda i: (0, 0))],
        out_specs=pl.BlockSpec((tm, MEM_WIDTH), lambda i: (i, 0)),
        out_shape=jax.ShapeDtypeStruct((n, MEM_WIDTH), F32),
        compiler_params=_params(("parallel",), blocks, _nbytes((tm, mem_kv.shape[0]), F32) * 4),
        name="mem_attn_prompt",
    )(z, mem_kv)


ONES_ROWS = 16


def _head_keys_kernel(c_ref, w_ref, kr_ref, o_ref):
    o_ref[0, :, :QK_NOPE_DIM] = _dot(c_ref[...], w_ref[...]).astype(o_ref.dtype)
    o_ref[0, :, QK_NOPE_DIM:] = kr_ref[...]


def head_keys(c, w_uk, kr, *, tm=2048):
    n, k = c.shape
    tm = _tile(n, tm)
    blocks = _nbytes((tm, k), BF16) + _nbytes((k, QK_NOPE_DIM), BF16) + 3 * _nbytes((tm, LANES), BF16)
    return pl.pallas_call(
        _head_keys_kernel,
        grid=(N_B_HEADS, n // tm),
        in_specs=[pl.BlockSpec((tm, k), lambda h, i: (i, 0)),
                  pl.BlockSpec((k, QK_NOPE_DIM), lambda h, i: (0, h)),
                  pl.BlockSpec((tm, ROPE_PAD), lambda h, i: (i, 0))],
        out_specs=pl.BlockSpec((1, tm, Q_HEAD_PAD), lambda h, i: (h, i, 0)),
        out_shape=jax.ShapeDtypeStruct((N_B_HEADS, n, Q_HEAD_PAD), BF16),
        compiler_params=_params(("parallel", "parallel"), blocks),
        name="head_keys",
    )(c, w_uk, kr)


def _head_values_t_kernel(w_ref, c_ref, o_ref):
    o_ref[0, :V_HEAD_DIM, :] = _dot_nt(w_ref[...], c_ref[...]).astype(o_ref.dtype)
    o_ref[0, V_HEAD_DIM:, :] = jnp.ones((ONES_ROWS, o_ref.shape[2]), o_ref.dtype)


def head_values_t(w_uv_t, c, *, tn=2048):
    n, k = c.shape
    tn = _tile(n, tn)
    rows = V_HEAD_DIM + ONES_ROWS
    blocks = _nbytes((V_HEAD_DIM, k), BF16) + _nbytes((tn, k), BF16) + _nbytes((rows, tn), BF16)
    return pl.pallas_call(
        _head_values_t_kernel,
        grid=(N_B_HEADS, n // tn),
        in_specs=[pl.BlockSpec((V_HEAD_DIM, k), lambda h, j: (h, 0)), pl.BlockSpec((tn, k), lambda h, j: (j, 0))],
        out_specs=pl.BlockSpec((1, rows, tn), lambda h, j: (h, 0, j)),
        out_shape=jax.ShapeDtypeStruct((N_B_HEADS, rows, n), BF16),
        compiler_params=_params(("parallel", "parallel"), blocks),
        name="head_values_t",
    )(w_uv_t, c)


def _mla_prompt_kernel(q_ref, kcat_ref, vt1_ref, o_ref, *, tq, hps):
    qi = pl.program_id(1)
    nv = V_HEAD_DIM
    qs = [q_ref[:, a * Q_HEAD_PAD:(a + 1) * Q_HEAD_PAD] for a in range(hps)]

    def step(kb, carry, masked):
        start = pl.multiple_of(kb * tq, tq)
        scores = [_dot_nt(kcat_ref[a, pl.ds(start, tq), :], qs[a]) for a in range(hps)]
        out = []
        for a in range(hps):
            m, acc = carry[a]
            s = scores[a]
            if masked:
                key = lax.broadcasted_iota(jnp.int32, (tq, tq), 0)
                qry = lax.broadcasted_iota(jnp.int32, (tq, tq), 1)
                s = jnp.where(key <= qry, s, NEG)
            m_new = jnp.maximum(m, jnp.max(s, axis=0, keepdims=True))
            alpha = jnp.exp(m - m_new)
            p = jnp.exp(s - m_new).astype(BF16)
            acc = alpha * acc + _dot(vt1_ref[a, :, pl.ds(start, tq)], p)
            out.append((m_new, acc))
        return tuple(out)

    init = tuple((jnp.full((1, tq), NEG, F32), jnp.zeros((vt1_ref.shape[1], tq), F32)) for _ in range(hps))
    carry = lax.fori_loop(0, qi, lambda kb, c: step(kb, c, False), init)
    carry = step(qi, carry, True)
    for a, (_, acc) in enumerate(carry):
        o_ref[:, a * nv:(a + 1) * nv] = (acc[:nv] / acc[nv:nv + 1]).T.astype(o_ref.dtype)


def mla_prompt(q, kcat, vt1, *, tq=512, heads_per_step=4):
    s_len = q.shape[0]
    tq = _tile(s_len, tq)
    hps = heads_per_step
    vrows = vt1.shape[1]
    blocks = _nbytes((tq, hps * Q_HEAD_PAD), BF16) + _nbytes((tq, hps * V_HEAD_DIM), BF16)
    resident = (hps * _nbytes((s_len, Q_HEAD_PAD), BF16) + hps * _nbytes((vrows, s_len), BF16)) // 2
    return pl.pallas_call(
        functools.partial(_mla_prompt_kernel, tq=tq, hps=hps),
        grid=(N_B_HEADS // hps, s_len // tq),
        in_specs=[pl.BlockSpec((tq, hps * Q_HEAD_PAD), lambda h, i: (i, h)),
                  pl.BlockSpec((hps, s_len, Q_HEAD_PAD), lambda h, i: (h, 0, 0), pipeline_mode=pl.Buffered(1)),
                  pl.BlockSpec((hps, vrows, s_len), lambda h, i: (h, 0, 0), pipeline_mode=pl.Buffered(1))],
        out_specs=pl.BlockSpec((tq, hps * V_HEAD_DIM), lambda h, i: (i, h)),
        out_shape=jax.ShapeDtypeStruct((s_len, N_B_HEADS * V_HEAD_DIM), BF16),
        compiler_params=_params(("parallel", "arbitrary"), blocks + 4 * hps * _nbytes((tq, tq), F32), resident),
        name="mla_prompt",
    )(q, kcat, vt1)


def _row_scores(q, k, bias):
    return jnp.sum(k * q[None], axis=-1, keepdims=True) * ATTN_SCALE + bias


def _dswa_sample_kernel(buf_ref, new_ref, q_ref, bias_a_ref, bias_b_ref, nbuf_ref, o_ref, lse_ref, *, d):
    nsub = buf_ref.shape[1]
    nh = HEADS_PER_GROUP
    t_new = new_ref.shape[1]
    for r_out in range(d):
        sh, r_in = divmod(r_out + t_new, d)
        if nsub - sh > 0:
            nbuf_ref[0, 0:nsub - sh, r_out] = buf_ref[0, sh:nsub, r_in]
        for a in range(max(nsub - sh, 0), nsub):
            nbuf_ref[0, a, r_out] = new_ref[0, a * d + r_out + t_new - nsub * d]
    k_new = new_ref[0, :, 0:nh, :]
    v_new = new_ref[0, :, nh:2 * nh, :]
    for i in range(t_new):
        q = q_ref[0, i]
        k_old = buf_ref[0, :, i % d, 0:nh, :]
        v_old = buf_ref[0, :, i % d, nh:2 * nh, :]
        s_a = _row_scores(q, k_old, bias_a_ref[i])
        s_b = _row_scores(q, k_new, bias_b_ref[i])
        m = jnp.maximum(jnp.max(s_a, axis=0), jnp.max(s_b, axis=0))
        e_a = jnp.exp(s_a - m[None])
        e_b = jnp.exp(s_b - m[None])
        l = jnp.sum(e_a, axis=0) + jnp.sum(e_b, axis=0)
        acc = jnp.sum(e_a * v_old, axis=0) + jnp.sum(e_b * v_new, axis=0)
        o_ref[0, i] = acc / l
        lse_ref[0, i] = m + jnp.log(l)


def dswa_sample(buf, new_kv, q, bias_a, bias_b, d):
    b, l_buf = buf.shape[:2]
    t_new = new_kv.shape[1]
    nsub = l_buf // d
    rows = 2 * HEADS_PER_GROUP
    bufv = buf.reshape(b, nsub, d, rows, HEAD_DIM)
    buf_spec = pl.BlockSpec((1, nsub, d, rows, HEAD_DIM), lambda i: (i, 0, 0, 0, 0))
    tok = lambda r: pl.BlockSpec((1, t_new, r, HEAD_DIM), lambda i: (i, 0, 0, 0))
    full = lambda a: pl.BlockSpec(a.shape, lambda i: (0,) * a.ndim)
    blocks = 2 * _nbytes((l_buf, rows, HEAD_DIM), F32) + _nbytes(bias_a.shape, F32)
    nbuf, o, lse = pl.pallas_call(
        functools.partial(_dswa_sample_kernel, d=d),
        grid=(b,),
        in_specs=[buf_spec, tok(rows), tok(HEADS_PER_GROUP), full(bias_a), full(bias_b)],
        out_specs=[buf_spec, tok(HEADS_PER_GROUP), tok(HEADS_PER_GROUP)],
        out_shape=[jax.ShapeDtypeStruct(bufv.shape, F32),
                   jax.ShapeDtypeStruct((b, t_new, HEADS_PER_GROUP, HEAD_DIM), F32),
                   jax.ShapeDtypeStruct((b, t_new, HEADS_PER_GROUP, HEAD_DIM), F32)],
        compiler_params=_params(("parallel",), blocks),
        name=f"dswa_sample_d{d}",
    )(bufv, new_kv, q, bias_a, bias_b)
    return nbuf.reshape(buf.shape), o, lse


def _dswa_sample_dense_kernel(buf_ref, new_ref, q_ref, bias_ref, nbuf_ref, o_ref, lse_ref):
    nh = HEADS_PER_GROUP
    l_buf, rows = buf_ref.shape[1], buf_ref.shape[2]
    t_new = new_ref.shape[1]
    ncol = (l_buf + t_new) * rows
    data, scores = [], []
    for j in range(q_ref.shape[0]):
        x, fresh = buf_ref[j], new_ref[j]
        nbuf_ref[j, 0:l_buf - t_new] = x[t_new:]
        nbuf_ref[j, l_buf - t_new:] = fresh
        src = jnp.concatenate([x, fresh], axis=0)
        swapped = pltpu.roll(src, nh, axis=1).reshape(ncol, HEAD_DIM).astype(BF16)
        data.append(src.reshape(ncol, HEAD_DIM).astype(BF16))
        scores.append(_dot_nt(q_ref[j].astype(BF16), swapped) * ATTN_SCALE + bias_ref[...])
    for j, s in enumerate(scores):
        m = jnp.max(s, axis=-1, keepdims=True)
        e = jnp.exp(s - m)
        l = jnp.sum(e, axis=-1, keepdims=True)
        o_ref[j] = _dot(e.astype(BF16), data[j]) / l
        lse_ref[j] = jnp.broadcast_to(m + jnp.log(l), o_ref.shape[1:])


def dswa_sample_dense(buf, new_kv, q, bias):
    b, l_buf, rows, _ = buf.shape
    seq_bytes = _nbytes((l_buf, rows, HEAD_DIM), F32)
    nb = max(1, min(8, (4 << 20) // seq_bytes))
    nb = nb if b % nb == 0 else 1
    lead = lambda a: pl.BlockSpec((nb,) + a.shape[1:], lambda i: (i,) + (0,) * (a.ndim - 1))
    blocks = 2 * nb * seq_bytes + _nbytes(bias.shape, F32)
    return pl.pallas_call(
        _dswa_sample_dense_kernel,
        grid=(b // nb,),
        in_specs=[lead(buf), lead(new_kv), lead(q), pl.BlockSpec(bias.shape, lambda i: (0, 0))],
        out_specs=[lead(buf), lead(q), lead(q)],
        out_shape=[jax.ShapeDtypeStruct(buf.shape, F32), jax.ShapeDtypeStruct(q.shape, F32),
                   jax.ShapeDtypeStruct(q.shape, F32)],
        compiler_params=_params(("parallel",), blocks, 2 * nb * seq_bytes),
        name=f"dswa_sample_dense_l{l_buf}",
    )(buf, new_kv, q, bias)


def _mem_attn_sample_kernel(kv_ref, q_ref, o_ref):
    nh = N_MEM_HEADS
    nq = q_ref.shape[1]
    n_mem, rows = kv_ref.shape[2], kv_ref.shape[3]
    ncol = n_mem * rows
    col_row = lax.broadcasted_iota(jnp.int32, (nq, ncol), 1) % rows
    q_head = lax.broadcasted_iota(jnp.int32, (nq, ncol), 0) // (nq // nh)
    own = col_row == q_head + nh
    data, scores = [], []
    for j in range(q_ref.shape[0]):
        x = kv_ref[0, j]
        swapped = pltpu.roll(x, nh, axis=1).reshape(ncol, HEAD_DIM).astype(BF16)
        data.append(x.reshape(ncol, HEAD_DIM).astype(BF16))
        scores.append(_dot_nt(q_ref[j].astype(BF16), swapped) * ATTN_SCALE)
    for j, s in enumerate(scores):
        s = jnp.where(own, s, NEG)
        e = jnp.exp(s - jnp.max(s, axis=-1, keepdims=True))
        o_ref[j] = _dot(e.astype(BF16), data[j]) / jnp.sum(e, axis=-1, keepdims=True)


def mem_attn_sample(cache, layer, q, *, seqs_per_step=8):
    _, b, n_mem, rows, _ = cache.shape
    nb = seqs_per_step if b % seqs_per_step == 0 else 1
    tok = pl.BlockSpec((nb,) + q.shape[1:], lambda i: (i, 0, 0))
    return pl.pallas_call(
        _mem_attn_sample_kernel,
        grid=(b // nb,),
        in_specs=[pl.BlockSpec((1, nb, n_mem, rows, HEAD_DIM), lambda i: (layer, i, 0, 0, 0)), tok],
        out_specs=tok,
        out_shape=jax.ShapeDtypeStruct(q.shape, F32),
        compiler_params=_params(("parallel",), nb * _nbytes((n_mem, rows, HEAD_DIM), F32),
                                2 * nb * _nbytes((n_mem, rows, HEAD_DIM), F32)),
        name="mem_attn_sample",
    )(cache, q)


def _mla_sample_kernel(pt_ref, q_ref, new_ref, *rest, n_pages, n_seq, t_new):
    del pt_ref
    page_refs = rest[:n_pages * n_seq]
    o_ref, m_ref, l_ref, acc_ref = rest[n_pages * n_seq:]
    c = pl.program_id(1)
    nq = q_ref.shape[1]

    @pl.when(c == 0)
    def _():
        m_ref[...] = jnp.full(m_ref.shape, NEG, F32)
        l_ref[...] = jnp.zeros(l_ref.shape, F32)
        acc_ref[...] = jnp.zeros(acc_ref.shape, F32)

    def update(j, s_parts, pv_fn):
        m_prev = m_ref[j]
        m_cur = jnp.max(functools.reduce(jnp.maximum, s_parts), axis=-1, keepdims=True)
        m_new = jnp.maximum(m_prev, m_cur)
        alpha = jnp.exp(m_prev - m_new)
        p = [jnp.exp(s - m_new) for s in s_parts]
        l_ref[j] = alpha * l_ref[j] + jnp.sum(functools.reduce(jnp.add, p), axis=-1, keepdims=True)
        m_ref[j] = m_new
        acc_ref[j] = alpha * acc_ref[j] + pv_fn([x.astype(BF16) for x in p])

    pages = [[r[0].astype(BF16) for r in page_refs[j * n_pages:(j + 1) * n_pages]] for j in range(n_seq)]
    scores = [[_dot(q_ref[j], kt) for kt in pages[j]] for j in range(n_seq)]
    for j in range(n_seq):
        update(j, scores[j], lambda p, kts=pages[j]: functools.reduce(
            jnp.add, [_dot_nt(pi, kt[:KV_LORA_RANK]) for pi, kt in zip(p, kts)]))

    @pl.when(c == pl.num_programs(1) - 1)
    def _():
        tok = lax.broadcasted_iota(jnp.int32, (nq, t_new), 0) // (nq // t_new)
        key = lax.broadcasted_iota(jnp.int32, (nq, t_new), 1)
        for j in range(n_seq):
            nr = new_ref[j].astype(BF16)
            s = jnp.where(key <= tok, _dot_nt(q_ref[j], nr), NEG)
            update(j, [s], lambda p, nr=nr: _dot(p[0], nr[:, :KV_LORA_RANK]))
            o_ref[j] = acc_ref[j] / l_ref[j]


def mla_sample(q, new_rows, cache_t, page_table):
    b, nq, row = q.shape
    t_new = new_rows.shape[1]
    n_seq_pages = page_table.shape[1]
    pps = PAGES_PER_STEP if n_seq_pages % PAGES_PER_STEP == 0 else 1
    nb = SEQS_PER_STEP if b % SEQS_PER_STEP == 0 else 1

    def page_spec(j, p):
        return pl.BlockSpec((1, row, PAGE_SIZE), lambda i, c, pt: (pt[i * nb + j, c * pps + p], 0, 0))

    blocks = (nb * pps * _nbytes((row, PAGE_SIZE), F32) + nb * _nbytes((nq, row), BF16)
              + nb * _nbytes((nq, KV_LORA_RANK), F32))
    return pl.pallas_call(
        functools.partial(_mla_sample_kernel, n_pages=pps, n_seq=nb, t_new=t_new),
        grid_spec=pltpu.PrefetchScalarGridSpec(
            num_scalar_prefetch=1,
            grid=(b // nb, n_seq_pages // pps),
            in_specs=[pl.BlockSpec((nb, nq, row), lambda i, c, pt: (i, 0, 0)),
                      pl.BlockSpec((nb, t_new, row), lambda i, c, pt: (i, 0, 0))]
                     + [page_spec(j, p) for j in range(nb) for p in range(pps)],
            out_specs=pl.BlockSpec((nb, nq, KV_LORA_RANK), lambda i, c, pt: (i, 0, 0)),
            scratch_shapes=[pltpu.VMEM((nb, nq, 1), F32), pltpu.VMEM((nb, nq, 1), F32),
                            pltpu.VMEM((nb, nq, KV_LORA_RANK), F32)]),
        out_shape=jax.ShapeDtypeStruct((b, nq, KV_LORA_RANK), F32),
        compiler_params=_params(("parallel", "arbitrary"), blocks,
                                nb * pps * _nbytes((row, PAGE_SIZE), BF16)),
        name="mla_sample",
    )(page_table, q, new_rows, *([cache_t] * (nb * pps)))


def _t5_bucket(dist):
    max_exact = N_BUCKETS // 2
    dd = jnp.maximum(dist, 1).astype(F32)
    large = max_exact + (jnp.log(dd / max_exact) / math.log(MAX_DISTANCE / max_exact)
                         * (N_BUCKETS - max_exact)).astype(jnp.int32)
    large = jnp.minimum(large, N_BUCKETS - 1)
    return jnp.where(dist < max_exact, dist, large)


def _group_bias(t5_bias, g):
    w, d = DSWA_GROUPS[g]
    offs = jnp.arange(w // d + 1, dtype=jnp.int32) * d
    return t5_bias[_t5_bucket(offs), g * HEADS_PER_GROUP:(g + 1) * HEADS_PER_GROUP].astype(F32)


def _banded(table, j):
    jmax = table.shape[0] - 1
    vals = table[jnp.clip(j, 0, jmax)]
    return jnp.where(((j >= 0) & (j <= jmax))[..., None], vals, NEG)


def _prompt_bias(table):
    blk, nh = table.shape[0] - 1, table.shape[1]
    n = 3 * blk - 1
    pad = jnp.full((blk - 1, nh), NEG, F32)
    u = jnp.concatenate([pad, table, pad], axis=0)
    skew = jnp.tile(u, (blk + 1, 1))[:blk * (n + 1)].reshape(blk, n + 1, nh)
    return jnp.moveaxis(skew[:, :2 * blk][:, ::-1], -1, 0)


def _sample_bias(table, d, t_new):
    blk = table.shape[0] - 1
    i = jnp.arange(t_new)
    j_a = blk - jnp.arange(blk)[None, :] + (i // d)[:, None]
    diff = i[:, None] - i[None, :]
    j_b = jnp.where((diff >= 0) & (diff % d == 0), diff // d, -1)
    lanes = lambda x: jnp.broadcast_to(x[..., None], x.shape + (HEAD_DIM,))
    return lanes(_banded(table, j_a)), lanes(_banded(table, j_b))


def _sample_bias_dense(table, d, t_new):
    blk, nh = table.shape[0] - 1, table.shape[1]
    w = blk * d
    stuffed = jnp.concatenate([table[:, None, :], jnp.full((blk + 1, d - 1, nh), NEG, F32)], axis=1)
    stuffed = stuffed.reshape((blk + 1) * d, nh)[:w + 1]
    pad = jnp.full((t_new - 1, nh), NEG, F32)
    rev = jnp.concatenate([pad, stuffed, pad], axis=0)[::-1]
    band = jnp.stack([rev[t_new - 1 - i:t_new - 1 - i + w + t_new] for i in range(t_new)])
    band = jnp.transpose(band, (2, 0, 1))[..., None]
    head = jnp.arange(nh)[:, None, None, None]
    sub = jnp.arange(2 * nh)[None, None, None, :]
    return jnp.where(sub == head + nh, band, NEG).reshape(nh * t_new, (w + t_new) * 2 * nh)


def _rope_tables(pos):
    half = QK_ROPE_DIM // 2
    inv_freq = ROPE_THETA ** (-jnp.arange(half, dtype=F32) / half)
    ang = pos.astype(F32)[:, None] * inv_freq[None, :]
    cos, sin = jnp.cos(ang), jnp.sin(ang)
    zero = jnp.zeros_like(cos)
    cc = jnp.concatenate([cos, cos, zero, zero], axis=-1)
    shi = jnp.concatenate([zero, sin, zero, zero], axis=-1)
    slo = jnp.concatenate([-sin, zero, zero, zero], axis=-1)
    return cc, shi, slo


def _pad_last(w, to):
    return jnp.pad(w, [(0, 0)] * (w.ndim - 1) + [(0, to - w.shape[-1])])


def _trunk(x, rope_tables, dswa_fn, mem_fn, mla_fn, p):
    z0 = norm_matmul(x, p['g_attn_pre'][0], p['w_a_in'])
    outs, lses = dswa_fn(z0)
    o_mem = mem_fn(0, z0, A_QKV_WIDTH // MEM_WIDTH)
    x = out_proj_groups(outs, lses, o_mem, p['w_a_out'][:GROUP_WIDTH], p['w_a_out'][GROUP_WIDTH:],
                        p['g_attn_post'][0], x)
    x = mlp(x, p['g_mlp_pre'][0], p['w_mlp_up'][0], p['w_mlp_down'][0], p['g_mlp_post'][0])
    raw = norm_matmul(x, p['g_kv_in'], p['w_kv_down'])
    rows, c_bf, kr_bf = kv_finish(raw, p['g_kv_latent'], rope_tables)
    z = norm_matmul(x, p['g_attn_pre'][1], p['w_b_in'])
    q = norm_matmul(z, p['g_q_latent'], p['w_q_up'], k_width=Q_LORA_RANK, rope_tables=rope_tables,
                    scale=MLA_SCALE, out_dtype=BF16, tn=2 * Q_HEAD_PAD)
    o_tok = mla_fn(q, rows, c_bf, kr_bf)
    o_mem = mem_fn(1, z, Q_LORA_RANK // MEM_WIDTH)
    nv = N_B_HEADS * V_HEAD_DIM
    x = out_proj(o_tok, o_mem, p['w_b_out'][:nv], p['w_b_out'][nv:], p['g_attn_post'][1], x)
    x = mlp(x, p['g_mlp_pre'][1], p['w_mlp_up'][1], p['w_mlp_down'][1], p['g_mlp_post'][1])
    return x, rows[:, :MLA_ROW], z0


def kernel(x_prompt, x_sample, mem_prompt, cache_swa_kv_w128, cache_swa_kv_w512, cache_swa_kv_w2048, cache_mla_kv, cache_mem_kv, page_table, t5_bias, g_attn_pre, g_attn_post, g_mlp_pre, g_mlp_post, g_mem, w_mem_kv, w_mlp_up, w_mlp_down, w_a_in, w_a_out, g_kv_in, w_kv_down, g_kv_latent, w_kv_up, w_b_in, g_q_latent, w_q_up, w_b_out):
    depth = g_attn_pre.shape[0]
    assert depth == 2 and w_a_in.shape[0] == 1 and w_b_in.shape[0] == 1
    bp, s_len, d_model = x_prompt.shape
    assert bp == 1
    bs, t_new, _ = x_sample.shape
    past = page_table.shape[1] * PAGE_SIZE
    bufs = (cache_swa_kv_w128, cache_swa_kv_w512, cache_swa_kv_w2048)
    for buf, (w, d) in zip(bufs, DSWA_GROUPS):
        assert buf.shape[2] == w and past >= w and s_len % (d * (w // d)) == 0 and s_len >= w

    w_q_up_pad = _pad_last(w_q_up[0].reshape(Q_LORA_RANK, N_B_HEADS, QK_NOPE_DIM + QK_ROPE_DIM),
                           Q_HEAD_PAD).reshape(Q_LORA_RANK, N_B_HEADS * Q_HEAD_PAD)
    p = {
        'g_attn_pre': g_attn_pre, 'g_attn_post': g_attn_post, 'g_mlp_pre': g_mlp_pre, 'g_mlp_post': g_mlp_post,
        'g_kv_in': g_kv_in, 'g_kv_latent': g_kv_latent, 'g_q_latent': g_q_latent[0],
        'w_a_in': w_a_in[0].astype(BF16), 'w_a_out': w_a_out[0].astype(BF16),
        'w_b_in': w_b_in[0].astype(BF16), 'w_b_out': w_b_out[0].astype(BF16),
        'w_mlp_up': w_mlp_up.astype(BF16), 'w_mlp_down': w_mlp_down.astype(BF16),
        'w_kv_down': _pad_last(w_kv_down, MLA_ROW_PAD).astype(BF16),
        'w_q_up': w_q_up_pad.astype(BF16),
    }
    w_uk = w_kv_up[..., :QK_NOPE_DIM]
    w_uv = w_kv_up[..., QK_NOPE_DIM:]
    w_uk_cat = w_uk.reshape(KV_LORA_RANK, -1).astype(BF16)
    w_uv_cat_t = w_uv.reshape(KV_LORA_RANK, -1).T.astype(BF16)
    w_uk_t = jnp.transpose(w_uk, (1, 2, 0)).astype(BF16)
    w_uv_h = jnp.transpose(w_uv, (1, 0, 2)).astype(BF16)
    tables = [_group_bias(t5_bias, g) for g in range(N_GROUPS)]

    mem_kv_prompt = [norm_matmul(mem_prompt[0], g_mem[l], w_mem_kv[l].astype(BF16)) for l in range(depth)]

    def dswa_p(z):
        parts = [dswa_prompt(z, g, _prompt_bias(tables[g])) for g in range(N_GROUPS)]
        return [o for o, _ in parts], [l for _, l in parts]

    def mla_p(q, rows, c_bf, kr_bf):
        return mla_prompt(q, head_keys(c_bf, w_uk_cat, kr_bf), head_values_t(w_uv_cat_t, c_bf))

    y_p, rows_p, z_p = _trunk(x_prompt[0], _rope_tables(jnp.arange(s_len, dtype=jnp.int32)), dswa_p,
                              lambda l, z, cb: mem_attn_prompt(z, cb, mem_kv_prompt[l]), mla_p, p)
    swa_p = []
    for g, (w, _) in enumerate(DSWA_GROUPS):
        k_g = z_p[s_len - w:, (N_GROUPS + g) * GROUP_WIDTH:(N_GROUPS + g + 1) * GROUP_WIDTH]
        v_g = z_p[s_len - w:, (2 * N_GROUPS + g) * GROUP_WIDTH:(2 * N_GROUPS + g + 1) * GROUP_WIDTH]
        kv_g = jnp.stack([k_g.reshape(w, HEADS_PER_GROUP, HEAD_DIM), v_g.reshape(w, HEADS_PER_GROUP, HEAD_DIM)], axis=1)
        swa_p.append(kv_g[None, None])
    mem_kv_out = jnp.stack(mem_kv_prompt).reshape(depth, 1, mem_prompt.shape[1], 2, N_MEM_HEADS, HEAD_DIM)

    ns = bs * t_new
    pos_s = past + (jnp.arange(ns, dtype=jnp.int32) % t_new)
    swa_s = []

    def heads(a, nh):
        return a.reshape(bs, t_new, nh, HEAD_DIM)

    def dswa_s(z):
        outs, lses = [], []
        for g, (w, d) in enumerate(DSWA_GROUPS):
            sec = lambda c: z[:, (c * N_GROUPS + g) * GROUP_WIDTH:(c * N_GROUPS + g + 1) * GROUP_WIDTH]
            new_kv = jnp.concatenate([heads(sec(1), HEADS_PER_GROUP), heads(sec(2), HEADS_PER_GROUP)], axis=2)
            buf = bufs[g][0].reshape(bs, w, 2 * HEADS_PER_GROUP, HEAD_DIM)
            q = heads(sec(0), HEADS_PER_GROUP)
            if w <= DENSE_SAMPLE_MAX_ROWS:
                q = jnp.swapaxes(q, 1, 2).reshape(bs, HEADS_PER_GROUP * t_new, HEAD_DIM)
                nbuf, o, lse = dswa_sample_dense(buf, new_kv, q, _sample_bias_dense(tables[g], d, t_new))
                o, lse = (jnp.swapaxes(a.reshape(bs, HEADS_PER_GROUP, t_new, HEAD_DIM), 1, 2) for a in (o, lse))
            else:
                bias_a, bias_b = _sample_bias(tables[g], d, t_new)
                nbuf, o, lse = dswa_sample(buf, new_kv, q, bias_a, bias_b, d)
            swa_s.append(nbuf.reshape(bufs[g].shape))
            outs.append(o.reshape(ns, GROUP_WIDTH))
            lses.append(lse.reshape(ns, GROUP_WIDTH))
        return outs, lses

    cache_mem = cache_mem_kv.reshape(depth, bs, cache_mem_kv.shape[2], 2 * N_MEM_HEADS, HEAD_DIM)

    def mem_s(l, z, cb):
        q = heads(z[:, cb * MEM_WIDTH:(cb + 1) * MEM_WIDTH], N_MEM_HEADS)
        q = jnp.swapaxes(q, 1, 2).reshape(bs, N_MEM_HEADS * t_new, HEAD_DIM)
        o = mem_attn_sample(cache_mem, l, q).reshape(bs, N_MEM_HEADS, t_new, HEAD_DIM)
        return jnp.swapaxes(o, 1, 2).reshape(ns, MEM_WIDTH)

    cache_mla_t = jnp.swapaxes(cache_mla_kv, 1, 2)

    def mla_s(q, rows, c_bf, kr_bf):
        q_lat = head_matmul(q, w_uk_t, x_block_stride=Q_HEAD_PAD // QK_NOPE_DIM, out_dtype=BF16)
        q3 = q.reshape(ns, N_B_HEADS, Q_HEAD_PAD)
        q_full = jnp.concatenate([q_lat.reshape(ns, N_B_HEADS, KV_LORA_RANK),
                                  q3[:, :, QK_NOPE_DIM:QK_NOPE_DIM + QK_ROPE_DIM]], axis=-1)
        q_full = q_full.reshape(bs, t_new * N_B_HEADS, MLA_ROW)
        new_rows = rows[:, :MLA_ROW].reshape(bs, t_new, MLA_ROW)
        o_lat = mla_sample(q_full, new_rows, cache_mla_t, page_table)
        return head_matmul(o_lat.reshape(ns, N_B_HEADS * KV_LORA_RANK), w_uv_h, out_dtype=BF16)

    y_s, rows_s, _ = _trunk(x_sample.reshape(ns, d_model), _rope_tables(pos_s), dswa_s, mem_s, mla_s, p)

    return (y_p[None], y_s.reshape(bs, t_new, d_model), swa_p[0], swa_p[1], swa_p[2],
            rows_p[None], mem_kv_out, swa_s[0], swa_s[1], swa_s[2], rows_s.reshape(bs, t_new, MLA_ROW))
```
